```python
import math
import numpy as np
import jax
import jax.numpy as jnp
from jax import lax

D_MODEL = 1024
BATCH = 16
SEQ = 4096
DEPTH = 4

CHUNK = 64
N_BRANCH = 4
BRANCH_WIDTH = D_MODEL // 2
HEAD_DIM = 64
RMS_EPS = 1e-6

SSD_INNER = BRANCH_WIDTH
SSD_HEAD_DIM = HEAD_DIM
SSD_HEADS = SSD_INNER // SSD_HEAD_DIM
SSD_GROUPS = 2
SSD_STATE = 64
SSD_CONV = 4
SSD_XBC = SSD_INNER + 2 * SSD_GROUPS * SSD_STATE

DSA_WIDTH = BRANCH_WIDTH
DSA_HEAD_DIM = HEAD_DIM
DSA_HEADS = DSA_WIDTH // DSA_HEAD_DIM
IDX_HEADS = 4
IDX_DIM = 32
DSA_MAX_TOPK = 256
DSA_Q_BLOCK = 32

REL_BUCKETS = 32
REL_MAX_DIST = 256

RWKV_WIDTH = BRANCH_WIDTH
RWKV_HEAD_DIM = HEAD_DIM
RWKV_HEADS = RWKV_WIDTH // RWKV_HEAD_DIM
RWKV_DECAY_LORA = 64
RWKV_ICLR_LORA = 64
RWKV_GATE_LORA = 128
RWKV_GN_EPS = 64e-5
RWKV_COLS = 3 * RWKV_WIDTH + RWKV_DECAY_LORA + RWKV_ICLR_LORA + RWKV_GATE_LORA

S5_WIDTH = BRANCH_WIDTH
S5_GROUP = 16
S5_GROUPS = S5_WIDTH // S5_GROUP
S5_STATE = 64

FFN_HIDDEN = -(-8 * D_MODEL // (3 * 256)) * 256

IN_SPLITS = (
    SSD_INNER, SSD_XBC, SSD_HEADS,
    DSA_WIDTH, DSA_WIDTH, DSA_WIDTH, IDX_HEADS * IDX_DIM, IDX_DIM, IDX_HEADS,
    RWKV_COLS,
    S5_WIDTH,
    N_BRANCH * D_MODEL,
)
IN_COLS = sum(IN_SPLITS)

kernel_name = "chunk_causal_hybrid_ssd_dsa_rwkv7_s5"


def _split_last(a, sizes):
    idx = np.cumsum(sizes)[:-1].tolist()
    return jnp.split(a, idx, axis=-1)


def _rmsnorm(x, g):
    xf = x.astype(jnp.float32)
    y = xf * lax.rsqrt(jnp.mean(xf * xf, axis=-1, keepdims=True) + RMS_EPS)
    return (y * g.astype(jnp.float32)).astype(x.dtype)


def _causal_depthwise_conv(x, w, bias):
    width, ch = w.shape
    y = lax.conv_general_dilated(x, w[:, None, :].astype(x.dtype), window_strides=(1,),
                                 padding=((width - 1, 0),), dimension_numbers=("NWC", "WIO", "NWC"),
                                 feature_group_count=ch)
    return y + bias.astype(y.dtype)


def _segsum(a):
    n = a.shape[-1]
    cs = jnp.cumsum(a, axis=-1)
    diff = cs[..., :, None] - cs[..., None, :]
    return jnp.where(jnp.tril(jnp.ones((n, n), dtype=bool)), diff, -jnp.inf)


def _ssd_scan(xs, a, bm, cm):
    b, t, h, p = xs.shape
    n = bm.shape[-1]
    c = t // CHUNK
    xs = xs.reshape(b, c, CHUNK, h, p)
    bm = bm.reshape(b, c, CHUNK, h, n)
    cm = cm.reshape(b, c, CHUNK, h, n)
    a = a.reshape(b, c, CHUNK, h).transpose(0, 3, 1, 2)
    a_cs = jnp.cumsum(a, axis=-1)
    decay_in = jnp.exp(_segsum(a))
    y_diag = jnp.einsum("bclhn,bcshn,bhcls,bcshp->bclhp", cm, bm, decay_in, xs)
    decay_to_end = jnp.exp(a_cs[..., -1:] - a_cs)
    chunk_states = jnp.einsum("bclhn,bhcl,bclhp->bchpn", bm, decay_to_end, xs)
    chunk_decay = jnp.exp(a_cs[..., -1])

    def step(state, inp):
        st, dec = inp
        return state * dec[..., None, None] + st, state

    _, prev = lax.scan(step, jnp.zeros((b, h, p, n), xs.dtype),
                       (jnp.moveaxis(chunk_states, 1, 0), jnp.moveaxis(chunk_decay, 2, 0)))
    prev = jnp.moveaxis(prev, 0, 1)
    y_off = jnp.einsum("bclhn,bchpn,bhcl->bclhp", cm, prev, jnp.exp(a_cs))
    return (y_diag + y_off).reshape(b, t, h, p)


def _ssd_mixer(z, xbc, dt_raw, conv_w, conv_b, dt_bias, a_log, d_skip, norm_g):
    b, t, _ = z.shape
    f32 = jnp.float32
    xbc = jax.nn.silu(_causal_depthwise_conv(xbc, conv_w, conv_b)).astype(f32)
    xs, bm, cm = _split_last(xbc, (SSD_INNER, SSD_GROUPS * SSD_STATE, SSD_GROUPS * SSD_STATE))
    xs = xs.reshape(b, t, SSD_HEADS, SSD_HEAD_DIM)
    rep = SSD_HEADS // SSD_GROUPS
    bm = jnp.repeat(bm.reshape(b, t, SSD_GROUPS, SSD_STATE), rep, axis=2)
    cm = jnp.repeat(cm.reshape(b, t, SSD_GROUPS, SSD_STATE), rep, axis=2)
    dt = jax.nn.softplus(dt_raw.astype(f32) + dt_bias.astype(f32))
    a = -jnp.exp(a_log.astype(f32))
    y = _ssd_scan(xs * dt[..., None], dt * a, bm, cm)
    y = y + d_skip.astype(f32)[:, None] * xs
    y = y.reshape(b, t, SSD_INNER) * jax.nn.silu(z.astype(f32))
    yg = y.reshape(b, t, SSD_GROUPS, SSD_INNER // SSD_GROUPS)
    yg = yg * lax.rsqrt(jnp.mean(yg * yg, axis=-1, keepdims=True) + RMS_EPS)
    return (yg.reshape(b, t, SSD_INNER) * norm_g.astype(f32)).astype(z.dtype)


def _t5_bucket(rel):
    half = REL_BUCKETS // 2
    exact = half // 2
    ret = jnp.where(rel > 0, half, 0)
    n = jnp.abs(rel)
    n_f = jnp.maximum(n, 1).astype(jnp.float32)
    large = exact + (jnp.log(n_f / exact) / math.log(REL_MAX_DIST / exact) * (half - exact)).astype(jnp.int32)
    large = jnp.minimum(large, half - 1)
    return ret + jnp.where(n < exact, n, large)


def _dsa_mixer(q, k, v, qi, ki, wi, rel_bias):
    b, t, h, dh = q.shape
    f32 = jnp.float32
    top_k = min(DSA_MAX_TOPK, t // 4)
    n_blk = t // DSA_Q_BLOCK
    key_chunk = jnp.arange(t) // CHUNK
    ki_f = ki.astype(f32)

    def one_block(start):
        qb = lax.dynamic_slice_in_dim(q, start, DSA_Q_BLOCK, axis=1)
        qib = lax.dynamic_slice_in_dim(qi, start, DSA_Q_BLOCK, axis=1).astype(f32)
        wib = lax.dynamic_slice_in_dim(wi, start, DSA_Q_BLOCK, axis=1).astype(f32)
        q_pos = start + jnp.arange(DSA_Q_BLOCK)
        s_h = jax.nn.relu(jnp.einsum("bqhe,bse->bqhs", qib, ki_f) * IDX_DIM ** -0.5)
        score = jnp.einsum("bqhs,bqh->bqs", s_h, wib) * IDX_HEADS ** -0.5
        admissible = key_chunk[None, :] <= (q_pos // CHUNK)[:, None]
        score = jnp.where(admissible[None], score, -jnp.inf)
        sel_score, sel_idx = lax.top_k(score, top_k)
        valid = jnp.isfinite(sel_score)
        k_sel = jax.vmap(lambda kb, ib: kb[ib])(k, sel_idx)
        v_sel = jax.vmap(lambda vb, ib: vb[ib])(v, sel_idx)
        logits = jnp.einsum("bqhd,bqkhd->bqhk", qb, k_sel).astype(f32) * dh ** -0.5
        bucket = _t5_bucket(sel_idx - q_pos[None, :, None])
        logits = logits + jnp.transpose(rel_bias[bucket], (0, 1, 3, 2)).astype(f32)
        logits = jnp.where(valid[:, :, None, :], logits, -jnp.inf)
        probs = jax.nn.softmax(logits, axis=-1)
        return jnp.einsum("bqhk,bqkhd->bqhd", probs.astype(v.dtype), v_sel)

    out = lax.map(one_block, jnp.arange(n_blk, dtype=jnp.int32) * DSA_Q_BLOCK)
    return out.transpose(1, 0, 2, 3, 4).reshape(b, t, h * dh)


def _wkv7_scan(r, decay, k, v, a, bb):
    bsz, t, h, n = r.shape

    def step(s, inp):
        r_t, w_t, k_t, v_t, a_t, b_t = inp
        sa = jnp.einsum("bhvk,bhk->bhv", s, a_t)
        s = s * w_t[:, :, None, :] + sa[..., None] * b_t[:, :, None, :] + v_t[..., None] * k_t[:, :, None, :]
        return s, jnp.einsum("bhvk,bhk->bhv", s, r_t)

    seq = tuple(jnp.moveaxis(z, 1, 0) for z in (r, decay, k, v, a, bb))
    _, y = lax.scan(step, jnp.zeros((bsz, h, n, n), jnp.float32), seq)
    return jnp.moveaxis(y, 0, 1)


def _rwkv7_mixer(cols, mu, w0, w2, a0, a2, g2, k_k, k_a, r_k, ln_g, ln_b):
    b, t, _ = cols.shape
    f32 = jnp.float32
    hd = (b, t, RWKV_HEADS, RWKV_HEAD_DIM)
    p = cols.astype(f32)
    p_prev = jnp.pad(p, ((0, 0), (1, 0), (0, 0)))[:, :-1]
    p = p + (p_prev - p) * mu.astype(f32)
    r, k, v, dw, da, dg = _split_last(p, (RWKV_WIDTH, RWKV_WIDTH, RWKV_WIDTH,
                                          RWKV_DECAY_LORA, RWKV_ICLR_LORA, RWKV_GATE_LORA))
    w = -jax.nn.softplus(-(w0.astype(f32) + jnp.tanh(dw) @ w2.astype(f32))) - 0.5
    decay = jnp.exp(-jnp.exp(w))
    a = jax.nn.sigmoid(a0.astype(f32) + da @ a2.astype(f32))
    g = jax.nn.sigmoid(dg) @ g2.astype(f32)
    kk = (k * k_k.astype(f32)).reshape(hd)
    kk = kk / jnp.maximum(jnp.linalg.norm(kk, axis=-1, keepdims=True), 1e-12)
    k = k * (1.0 + (a - 1.0) * k_a.astype(f32))
    rh, kh, vh, ah = r.reshape(hd), k.reshape(hd), v.reshape(hd), a.reshape(hd)
    y = _wkv7_scan(rh, decay.reshape(hd), kh, vh, -kk, kk * ah)
    mean = jnp.mean(y, axis=-1, keepdims=True)
    var = jnp.mean(jnp.square(y - mean), axis=-1, keepdims=True)
    y = ((y - mean) * lax.rsqrt(var + RWKV_GN_EPS)).reshape(b, t, RWKV_WIDTH)
    y = y * ln_g.astype(f32) + ln_b.astype(f32)
    bonus = jnp.sum(rh * kh * r_k.astype(f32), axis=-1, keepdims=True) * vh
    y = y + bonus.reshape(b, t, RWKV_WIDTH)
    return (y * g).astype(cols.dtype)


def _s5_mixer(u, a_re, a_im, b_re, b_im, c_re, c_im, d_skip, log_dt, glu_w, glu_b):
    bsz, t, _ = u.shape
    f32 = jnp.float32
    uf = u.astype(f32)
    ug = uf.reshape(bsz, t, S5_GROUPS, S5_GROUP)
    a = lax.complex(a_re.astype(f32), a_im.astype(f32))
    dt = jnp.exp(log_dt.astype(f32))[:, None]
    a_bar = jnp.exp(a * dt)
    b_bar = ((a_bar - 1.0) / a)[..., None] * lax.complex(b_re.astype(f32), b_im.astype(f32))
    bu = lax.complex(jnp.einsum("btgj,gpj->btgp", ug, jnp.real(b_bar)),
                     jnp.einsum("btgj,gpj->btgp", ug, jnp.imag(b_bar)))

    def combine(e1, e2):
        a1, b1 = e1
        a2, b2 = e2
        return a1 * a2, a2 * b1 + b2

    a_seq = jnp.broadcast_to(a_bar, (1, t, S5_GROUPS, S5_STATE))
    _, states = lax.associative_scan(combine, (a_seq, bu), axis=1)
    y = (jnp.einsum("btgp,gjp->btgj", jnp.real(states), c_re.astype(f32))
         - jnp.einsum("btgp,gjp->btgj", jnp.imag(states), c_im.astype(f32)))
    y = y.reshape(bsz, t, S5_WIDTH) + d_skip.astype(f32) * uf
    y = jax.nn.gelu(y)
    y = y * jax.nn.sigmoid(y @ glu_w.astype(f32) + glu_b.astype(f32))
    return y.astype(u.dtype)


def _swiglu(h, w1, w3, w2):
    return (jax.nn.silu(h @ w1) * (h @ w3)) @ w2


def setup_inputs(seed: int = 0) -> dict:
    key = jax.random.key(seed)
    ks = iter(jax.random.split(key, 40))
    f32 = jnp.float32
    L = DEPTH

    def nrm(shape, scale):
        return jax.random.normal(next(ks), shape, f32) * scale

    def uni(shape, lo, hi):
        return jax.random.uniform(next(ks), shape, f32, lo, hi)

    x = nrm((BATCH, SEQ, D_MODEL), 1.0)
    rel_bias = nrm((REL_BUCKETS, DSA_HEADS), 0.5)
    norm_mix_g = 1.0 + nrm((L, D_MODEL), 0.02)
    w_in = nrm((L, D_MODEL, IN_COLS), D_MODEL ** -0.5)
    ssd_conv_w = nrm((L, SSD_CONV, SSD_XBC), SSD_CONV ** -0.5)
    ssd_conv_b = nrm((L, SSD_XBC), 0.02)
    dt0 = jnp.exp(uni((L, SSD_HEADS), math.log(1e-3), math.log(1e-1)))
    ssd_dt_bias = dt0 + jnp.log(-jnp.expm1(-dt0))
    ssd_a_log = jnp.log(uni((L, SSD_HEADS), 1.0, 16.0))
    ssd_d = 1.0 + nrm((L, SSD_HEADS), 0.1)
    ssd_norm_g = 1.0 + nrm((L, SSD_INNER), 0.02)
    rwkv_mu = uni((L, RWKV_COLS), 0.0, 1.0)
    rwkv_w0 = uni((L, RWKV_WIDTH), -6.0, -1.0)
    rwkv_w2 = nrm((L, RWKV_DECAY_LORA, RWKV_WIDTH), 0.1)
    rwkv_a0 = nrm((L, RWKV_WIDTH), 0.1)
    rwkv_a2 = nrm((L, RWKV_ICLR_LORA, RWKV_WIDTH), 0.1)
    rwkv_g2 = nrm((L, RWKV_GATE_LORA, RWKV_WIDTH), RWKV_GATE_LORA ** -0.5)
    rwkv_k_k = 0.85 + nrm((L, RWKV_WIDTH), 0.05)
    rwkv_k_a = 1.0 + nrm((L, RWKV_WIDTH), 0.05)
    rwkv_r_k = nrm((L, RWKV_HEADS, RWKV_HEAD_DIM), 0.1)
    rwkv_ln_g = 1.0 + nrm((L, RWKV_WIDTH), 0.02)
    rwkv_ln_b = nrm((L, RWKV_WIDTH), 0.02)
    s5_a_re = -0.5 + nrm((L, S5_GROUPS, S5_STATE), 0.01)
    s5_a_im = jnp.pi * jnp.arange(S5_STATE, dtype=f32) + nrm((L, S5_GROUPS, S5_STATE), 0.01)
    s5_b_re = nrm((L, S5_GROUPS, S5_STATE, S5_GROUP), (2 * S5_GROUP) ** -0.5)
    s5_b_im = nrm((L, S5_GROUPS, S5_STATE, S5_GROUP), (2 * S5_GROUP) ** -0.5)
    s5_c_re = nrm((L, S5_GROUPS, S5_GROUP, S5_STATE), 0.25)
    s5_c_im = nrm((L, S5_GROUPS, S5_GROUP, S5_STATE), 0.25)
    s5_d = nrm((L, S5_WIDTH), 1.0)
    s5_log_dt = uni((L, S5_GROUPS), math.log(1e-3), math.log(1e-1))
    s5_glu_w = nrm((L, S5_WIDTH, S5_WIDTH), S5_WIDTH ** -0.5)
    s5_glu_b = nrm((L, S5_WIDTH), 0.02)
    w_branch = nrm((L, N_BRANCH, BRANCH_WIDTH, D_MODEL), BRANCH_WIDTH ** -0.5)
    w_out = nrm((L, D_MODEL, D_MODEL), D_MODEL ** -0.5)
    norm_ffn_g = 1.0 + nrm((L, D_MODEL), 0.02)
    ffn_w1 = nrm((L, D_MODEL, FFN_HIDDEN), D_MODEL ** -0.5)
    ffn_w3 = nrm((L, D_MODEL, FFN_HIDDEN), D_MODEL ** -0.5)
    ffn_w2 = nrm((L, FFN_HIDDEN, D_MODEL), FFN_HIDDEN ** -0.5)
    norm_final_g = 1.0 + nrm((D_MODEL,), 0.02)
    return {
        "x": x, "rel_bias": rel_bias, "norm_mix_g": norm_mix_g, "w_in": w_in,
        "ssd_conv_w": ssd_conv_w, "ssd_conv_b": ssd_conv_b, "ssd_dt_bias": ssd_dt_bias,
        "ssd_a_log": ssd_a_log, "ssd_d": ssd_d, "ssd_norm_g": ssd_norm_g,
        "rwkv_mu": rwkv_mu, "rwkv_w0": rwkv_w0, "rwkv_w2": rwkv_w2, "rwkv_a0": rwkv_a0,
        "rwkv_a2": rwkv_a2, "rwkv_g2": rwkv_g2, "rwkv_k_k": rwkv_k_k, "rwkv_k_a": rwkv_k_a,
        "rwkv_r_k": rwkv_r_k, "rwkv_ln_g": rwkv_ln_g, "rwkv_ln_b": rwkv_ln_b,
        "s5_a_re": s5_a_re, "s5_a_im": s5_a_im, "s5_b_re": s5_b_re, "s5_b_im": s5_b_im,
        "s5_c_re": s5_c_re, "s5_c_im": s5_c_im, "s5_d": s5_d, "s5_log_dt": s5_log_dt,
        "s5_glu_w": s5_glu_w, "s5_glu_b": s5_glu_b, "w_branch": w_branch, "w_out": w_out,
        "norm_ffn_g": norm_ffn_g, "ffn_w1": ffn_w1, "ffn_w3": ffn_w3, "ffn_w2": ffn_w2,
        "norm_final_g": norm_final_g,
    }


def reference(x, rel_bias, norm_mix_g, w_in, ssd_conv_w, ssd_conv_b, ssd_dt_bias, ssd_a_log, ssd_d,
              ssd_norm_g, rwkv_mu, rwkv_w0, rwkv_w2, rwkv_a0, rwkv_a2, rwkv_g2, rwkv_k_k, rwkv_k_a,
              rwkv_r_k, rwkv_ln_g, rwkv_ln_b, s5_a_re, s5_a_im, s5_b_re, s5_b_im, s5_c_re, s5_c_im,
              s5_d, s5_log_dt, s5_glu_w, s5_glu_b, w_branch, w_out, norm_ffn_g, ffn_w1, ffn_w3,
              ffn_w2, norm_final_g):
    b, t, _ = x.shape
    f32 = jnp.float32
    for i in range(DEPTH):
        h = _rmsnorm(x, norm_mix_g[i])
        proj = h @ w_in[i]
        (z, xbc, dt_raw, q, k, v, qi, ki, wi, rwkv_cols, s5_u, gate_cols) = _split_last(proj, IN_SPLITS)
        y_a = _ssd_mixer(z, xbc, dt_raw, ssd_conv_w[i], ssd_conv_b[i], ssd_dt_bias[i],
                         ssd_a_log[i], ssd_d[i], ssd_norm_g[i])
        y_b = _dsa_mixer(q.reshape(b, t, DSA_HEADS, DSA_HEAD_DIM), k.reshape(b, t, DSA_HEADS, DSA_HEAD_DIM),
                         v.reshape(b, t, DSA_HEADS, DSA_HEAD_DIM), qi.reshape(b, t, IDX_HEADS, IDX_DIM),
                         ki, wi, rel_bias).astype(x.dtype)
        y_c = _rwkv7_mixer(rwkv_cols, rwkv_mu[i], rwkv_w0[i], rwkv_w2[i], rwkv_a0[i], rwkv_a2[i],
                           rwkv_g2[i], rwkv_k_k[i], rwkv_k_a[i], rwkv_r_k[i], rwkv_ln_g[i], rwkv_ln_b[i])
        y_d = _s5_mixer(s5_u, s5_a_re[i], s5_a_im[i], s5_b_re[i], s5_b_im[i], s5_c_re[i], s5_c_im[i],
                        s5_d[i], s5_log_dt[i], s5_glu_w[i], s5_glu_b[i])
        merged = jnp.zeros((b, t, D_MODEL), f32)
        for n, y_n in enumerate((y_a, y_b, y_c, y_d)):
            gate = jax.nn.sigmoid(gate_cols[..., n * D_MODEL:(n + 1) * D_MODEL].astype(f32))
            merged = merged + gate * (y_n @ w_branch[i, n]).astype(f32)
        x = x + (merged.astype(x.dtype) @ w_out[i]).astype(x.dtype)
        h = _rmsnorm(x, norm_ffn_g[i])
        x = x + _swiglu(h, ffn_w1[i], ffn_w3[i], ffn_w2[i]).astype(x.dtype)
    return _rmsnorm(x, norm_final_g)
```

```python
import functools
import math

import jax
import jax.numpy as jnp
import numpy as np
from jax import lax
from jax.experimental import pallas as pl
from jax.experimental.pallas import tpu as pltpu

D_MODEL = 1024
DEPTH = 4
CHUNK = 64
N_BRANCH = 4
BRANCH_WIDTH = 512
HEAD_DIM = 64
N_HEADS = 8
RMS_EPS = 1e-6

SSD_GROUPS = 2
SSD_STATE = 64
SSD_CONV = 4
SSD_XBC = 768

IDX_HEADS = 4
IDX_DIM = 32
DSA_MAX_TOPK = 256
REL_BUCKETS = 32
REL_MAX_DIST = 256

RWKV_COLS = 1792
RWKV_GN_EPS = 64e-5

S5_GROUP = 16
S5_GROUPS = 32
S5_STATE = 64

FFN_HIDDEN = 2816

LANES = 128
SUBLANES = 8
VMEM_LIMIT = 56 * 1024 * 1024

F32 = jnp.float32
BF16 = jnp.bfloat16
HIGHEST = lax.Precision.HIGHEST


def _params(*sem):
    return pltpu.CompilerParams(dimension_semantics=sem, vmem_limit_bytes=VMEM_LIMIT)


def _resident(shape):
    nd = len(shape)
    return pl.BlockSpec(shape, lambda *_: (0,) * nd, pipeline_mode=pl.Buffered(1))


def _dot(a, b):
    return jnp.dot(a, b, preferred_element_type=F32)


def _dot_exact(a, b):
    return jnp.dot(a, b, preferred_element_type=F32, precision=HIGHEST)


def _sigmoid(x):
    return 1.0 / (1.0 + jnp.exp(-x))


def _silu(x):
    return x * _sigmoid(x)


def _softplus(x):
    return jnp.maximum(x, 0.0) + jnp.log(1.0 + jnp.exp(-jnp.abs(x)))


PROJ_SEGS = (
    ("z", 512, F32), ("xbc", 768, F32), ("q", 512, BF16), ("k", 512, BF16), ("v", 512, BF16),
    ("qi", 128, F32), ("small", 128, F32), ("rwkv", RWKV_COLS, F32), ("s5", 512, F32),
    ("gates", 4096, F32),
)
PROJ_COLS = sum(s[1] for s in PROJ_SEGS)


def _pack_w_in(w):
    o = np.cumsum((0, 512, 768, 8, 512, 512, 512, 128, 32, 4, RWKV_COLS, 512, 4096))
    seg = lambda i: w[:, o[i]:o[i + 1]]
    small = jnp.concatenate([seg(2), seg(7), seg(8), jnp.zeros((D_MODEL, LANES - 44), w.dtype)], axis=1)
    parts = [seg(0), seg(1), seg(3), seg(4), seg(5), seg(6), small, seg(9), seg(10), seg(11)]
    return jnp.concatenate(parts, axis=1).astype(BF16)


def _proj_kernel(x_ref, g_ref, w_ref, *out_refs):
    x = x_ref[...]
    ms = jnp.mean(x * x, axis=-1, keepdims=True)
    h = (x * lax.rsqrt(ms + RMS_EPS) * g_ref[...]).astype(BF16)
    off = 0
    for o_ref, (_, n, _) in zip(out_refs, PROJ_SEGS):
        o_ref[...] = _dot(h, w_ref[:, off:off + n]).astype(o_ref.dtype)
        off += n


def _proj(x2, g, w_packed, tm=256):
    m = x2.shape[0]
    tm = min(tm, m)
    return pl.pallas_call(
        _proj_kernel,
        grid=(m // tm,),
        in_specs=[pl.BlockSpec((tm, D_MODEL), lambda i: (i, 0)),
                  _resident((1, D_MODEL)), _resident((D_MODEL, PROJ_COLS))],
        out_specs=[pl.BlockSpec((tm, n), lambda i: (i, 0)) for _, n, _ in PROJ_SEGS],
        out_shape=[jax.ShapeDtypeStruct((m, n), dt) for _, n, dt in PROJ_SEGS],
        compiler_params=_params("parallel"),
        name="proj",
    )(x2, g.reshape(1, D_MODEL), w_packed)


def _merge_kernel(x_ref, ya_ref, yb_ref, yc_ref, gc_ref, yd_ref, gate_ref, wb_ref, wo_ref, o_ref):
    ys = (ya_ref[...], yb_ref[...], (yc_ref[...] * gc_ref[...]).astype(BF16), yd_ref[...])
    merged = None
    for n, y in enumerate(ys):
        gate = _sigmoid(gate_ref[:, n * D_MODEL:(n + 1) * D_MODEL])
        term = gate * _dot(y, wb_ref[n])
        merged = term if merged is None else merged + term
    o_ref[...] = x_ref[...] + _dot(merged.astype(BF16), wo_ref[...])


def _merge(x2, y_a, y_b, y_c, g_c, y_d, gates, wb, wo, tm=256):
    m = x2.shape[0]
    tm = min(tm, m)
    row = lambda n: pl.BlockSpec((tm, n), lambda i: (i, 0))
    return pl.pallas_call(
        _merge_kernel,
        grid=(m // tm,),
        in_specs=[row(D_MODEL)] + [row(BRANCH_WIDTH)] * 5 + [row(N_BRANCH * D_MODEL),
                  _resident((N_BRANCH, BRANCH_WIDTH, D_MODEL)), _resident((D_MODEL, D_MODEL))],
        out_specs=row(D_MODEL),
        out_shape=jax.ShapeDtypeStruct((m, D_MODEL), F32),
        compiler_params=_params("parallel"),
        name="merge",
    )(x2, y_a, y_b, y_c, g_c, y_d, gates, wb, wo)


def _ffn_kernel(x_ref, g_ref, w1_ref, w3_ref, w2_ref, gf_ref, o_ref, *, final_norm):
    x = x_ref[...]
    ms = jnp.mean(x * x, axis=-1, keepdims=True)
    h = (x * lax.rsqrt(ms + RMS_EPS) * g_ref[...]).astype(BF16)
    hid = _silu(_dot(h, w1_ref[...])) * _dot(h, w3_ref[...])
    x = x + _dot(hid.astype(BF16), w2_ref[...])
    if final_norm:
        ms = jnp.mean(x * x, axis=-1, keepdims=True)
        x = x * lax.rsqrt(ms + RMS_EPS) * gf_ref[...]
    o_ref[...] = x


def _ffn(x2, g, w1, w3, w2, g_final, final_norm, tm=256):
    m = x2.shape[0]
    tm = min(tm, m)
    row = pl.BlockSpec((tm, D_MODEL), lambda i: (i, 0))
    return pl.pallas_call(
        functools.partial(_ffn_kernel, final_norm=final_norm),
        grid=(m // tm,),
        in_specs=[row, _resident((1, D_MODEL)), _resident((D_MODEL, FFN_HIDDEN)),
                  _resident((D_MODEL, FFN_HIDDEN)), _resident((FFN_HIDDEN, D_MODEL)),
                  _resident((1, D_MODEL))],
        out_specs=row,
        out_shape=jax.ShapeDtypeStruct((m, D_MODEL), F32),
        compiler_params=_params("parallel"),
        name="ffn",
    )(x2, g.reshape(1, D_MODEL), w1, w3, w2, g_final.reshape(1, D_MODEL))


def _head_expand():
    e = np.zeros((LANES, BRANCH_WIDTH), np.float32)
    for h in range(N_HEADS):
        e[h, h * HEAD_DIM:(h + 1) * HEAD_DIM] = 1.0
    return jnp.asarray(e)


def _head_select():
    s = np.zeros((N_HEADS, BRANCH_WIDTH), np.float32)
    for h in range(N_HEADS):
        s[h, h * HEAD_DIM] = 1.0
    return jnp.asarray(s)


def _ssd_kernel(z_ref, xbc_ref, xprev_ref, small_ref, cw_ref, cb_ref, dtb_ref, alog_ref, dsk_ref,
                ng_ref, exp_ref, sel_ref, o_ref, xpad_scr, state_scr, y_scr, *, L):
    c = pl.program_id(1)

    @pl.when(c == 0)
    def _():
        state_scr[...] = jnp.zeros_like(state_scr)

    xpad_scr[0:SUBLANES, :] = jnp.where(c == 0, 0.0, xprev_ref[0])
    xpad_scr[SUBLANES:SUBLANES + L, :] = xbc_ref[0]
    acc = cb_ref[...]
    for j in range(SSD_CONV):
        lo = SUBLANES - (SSD_CONV - 1) + j
        acc = acc + cw_ref[j:j + 1, :] * xpad_scr[lo:lo + L, :]
    xc = _silu(acc)
    xs = xc[:, :BRANCH_WIDTH]

    dt = _softplus(_dot_exact(small_ref[0], exp_ref[...]) + dtb_ref[...])
    a = dt * (-jnp.exp(alog_ref[...]))
    rows = lax.broadcasted_iota(jnp.int32, (L, L), 0)
    cols = lax.broadcasted_iota(jnp.int32, (L, L), 1)
    causal = rows >= cols
    acs = _dot_exact(causal.astype(F32), a)
    acs_t = lax.dot_general(sel_ref[...], acs, (((1,), (1,)), ((), ())),
                            precision=HIGHEST, preferred_element_type=F32)
    xdt = xs * dt

    for g in range(SSD_GROUPS):
        b_g = xc[:, BRANCH_WIDTH + g * SSD_STATE:BRANCH_WIDTH + (g + 1) * SSD_STATE].astype(BF16)
        c_g = xc[:, BRANCH_WIDTH + (SSD_GROUPS + g) * SSD_STATE:
                 BRANCH_WIDTH + (SSD_GROUPS + g + 1) * SSD_STATE].astype(BF16)
        cb = lax.dot_general(c_g, b_g, (((1,), (1,)), ((), ())), preferred_element_type=F32)
        for hh in range(N_HEADS // SSD_GROUPS):
            h = g * (N_HEADS // SSD_GROUPS) + hh
            sl = slice(h * HEAD_DIM, (h + 1) * HEAD_DIM)
            col = acs[:, h * HEAD_DIM:h * HEAD_DIM + 1]
            row = acs_t[h:h + 1, :]
            dec = jnp.where(causal, jnp.exp(jnp.minimum(col - row, 0.0)), 0.0)
            x_h = xdt[:, sl]
            st = state_scr[h]
            y_h = _dot((cb * dec).astype(BF16), x_h.astype(BF16))
            y_h = y_h + _dot(c_g, st.astype(BF16)) * jnp.exp(col)
            y_scr[:, sl] = y_h
            last = acs[L - 1:L, h * HEAD_DIM:h * HEAD_DIM + 1]
            x_w = (x_h * jnp.exp(last - col)).astype(BF16)
            state_scr[h] = st * jnp.exp(last) + lax.dot_general(
                b_g, x_w, (((0,), (0,)), ((), ())), preferred_element_type=F32)

    y = y_scr[...] + dsk_ref[...] * xs
    y = y * _silu(z_ref[0])
    half = BRANCH_WIDTH // SSD_GROUPS
    for g in range(SSD_GROUPS):
        yg = y[:, g * half:(g + 1) * half]
        yg = yg * lax.rsqrt(jnp.mean(yg * yg, axis=-1, keepdims=True) + RMS_EPS)
        o_ref[0, :, g * half:(g + 1) * half] = (yg * ng_ref[:, g * half:(g + 1) * half]).astype(o_ref.dtype)


def _ssd(z, xbc, small, conv_w, conv_b, dt_bias, a_log, d_skip, norm_g, L=64):
    b, t, _ = z.shape
    rep = lambda p: jnp.repeat(p.astype(F32), HEAD_DIM).reshape(1, BRANCH_WIDTH)
    blk = lambda n: pl.BlockSpec((1, L, n), lambda i, j: (i, j, 0))
    prev = pl.BlockSpec((1, SUBLANES, SSD_XBC),
                        lambda i, j: (i, jnp.maximum(j * (L // SUBLANES) - 1, 0), 0))
    return pl.pallas_call(
        functools.partial(_ssd_kernel, L=L),
        grid=(b, t // L),
        in_specs=[blk(BRANCH_WIDTH), blk(SSD_XBC), prev, blk(LANES),
                  _resident((SSD_CONV, SSD_XBC)), _resident((1, SSD_XBC)),
                  _resident((1, BRANCH_WIDTH)), _resident((1, BRANCH_WIDTH)), _resident((1, BRANCH_WIDTH)),
                  _resident((1, BRANCH_WIDTH)), _resident((LANES, BRANCH_WIDTH)),
                  _resident((N_HEADS, BRANCH_WIDTH))],
        out_specs=blk(BRANCH_WIDTH),
        out_shape=jax.ShapeDtypeStruct((b, t, BRANCH_WIDTH), BF16),
        scratch_shapes=[pltpu.VMEM((SUBLANES + L, SSD_XBC), F32),
                        pltpu.VMEM((N_HEADS, SSD_STATE, HEAD_DIM), F32),
                        pltpu.VMEM((L, BRANCH_WIDTH), F32)],
        compiler_params=_params("parallel", "arbitrary"),
        name="ssd",
    )(z, xbc, xbc, small, conv_w, conv_b.reshape(1, SSD_XBC), rep(dt_bias), rep(a_log), rep(d_skip),
      norm_g.reshape(1, BRANCH_WIDTH), _head_expand(), _head_select())


S5_CHUNK = 64
S5_ROW = S5_CHUNK * S5_GROUP


def _s5_tables(a_re, a_im, b_re, b_im, c_re, c_im, log_dt):
    L = S5_CHUNK
    f = lambda v: v.astype(F32)
    a_re, a_im, b_re, b_im, c_re, c_im = map(f, (a_re, a_im, b_re, b_im, c_re, c_im))
    dt = jnp.exp(f(log_dt))[:, None]
    lr, li = a_re * dt, a_im * dt
    tau = jnp.arange(L + 1, dtype=F32)[None, :, None]
    mag = jnp.exp(lr[:, None, :] * tau)
    pr, pi = mag * jnp.cos(li[:, None, :] * tau), mag * jnp.sin(li[:, None, :] * tau)
    abr, abi = pr[:, 1], pi[:, 1]
    den = a_re * a_re + a_im * a_im
    qr = ((abr - 1.0) * a_re + abi * a_im) / den
    qi = (abi * a_re - (abr - 1.0) * a_im) / den
    bbr = qr[..., None] * b_re - qi[..., None] * b_im
    bbi = qr[..., None] * b_im + qi[..., None] * b_re
    pbr = pr[..., None] * bbr[:, None] - pi[..., None] * bbi[:, None]
    pbi = pr[..., None] * bbi[:, None] + pi[..., None] * bbr[:, None]
    kern = (jnp.einsum("gop,gtpi->gtoi", c_re, pbr[:, :L], precision=HIGHEST)
            - jnp.einsum("gop,gtpi->gtoi", c_im, pbi[:, :L], precision=HIGHEST))
    lag = jnp.arange(L)[None, :] - jnp.arange(L)[:, None]
    tz = jnp.where((lag >= 0)[None, :, :, None, None], kern[:, jnp.maximum(lag, 0)], 0.0)
    tz = tz.transpose(0, 1, 4, 2, 3).reshape(S5_GROUPS, S5_ROW, S5_ROW)
    rev = L - 1 - jnp.arange(L)
    ws = jnp.concatenate([pbr[:, rev], pbi[:, rev]], axis=2)
    ws = ws.transpose(0, 1, 3, 2).reshape(S5_GROUPS, S5_ROW, 2 * S5_STATE)
    cpr = c_re[:, None] * pr[:, 1:, None, :] - c_im[:, None] * pi[:, 1:, None, :]
    cpi = c_re[:, None] * pi[:, 1:, None, :] + c_im[:, None] * pr[:, 1:, None, :]
    wy = jnp.concatenate([cpr, -cpi], axis=3).transpose(0, 3, 1, 2).reshape(S5_GROUPS, 2 * S5_STATE, S5_ROW)
    a1 = jnp.concatenate([pr[:, L], pr[:, L]], axis=1)[:, None, :]
    a2 = jnp.concatenate([-pi[:, L], pi[:, L]], axis=1)[:, None, :]
    return tz.astype(BF16), ws.astype(BF16), wy.astype(BF16), a1, a2


def _s5_kernel(u_ref, tz_ref, ws_ref, wy_ref, a1_ref, a2_ref, y_ref, sl_scr, x_scr, *, nb, nc):
    u = u_ref[0]
    y_local = _dot(u, tz_ref[0])
    sl_scr[...] = _dot(u, ws_ref[0])
    a1, a2 = a1_ref[0], a2_ref[0]

    def step(c, x):
        r = pl.ds(pl.multiple_of(c * nb, SUBLANES), nb)
        x_scr[r, :] = x
        return a1 * x + a2 * pltpu.roll(x, S5_STATE, axis=1) + sl_scr[r, :]

    lax.fori_loop(0, nc, step, jnp.zeros((nb, 2 * S5_STATE), F32))
    y_ref[0] = y_local + _dot(x_scr[...].astype(BF16), wy_ref[0])


def _s5_out_kernel(y_ref, u_ref, d_ref, w_ref, b_ref, o_ref):
    y = y_ref[...] + d_ref[...] * u_ref[...]
    y = 0.5 * y * (1.0 + jnp.tanh(math.sqrt(2.0 / math.pi) * (y + 0.044715 * (y * y * y))))
    o_ref[...] = (y * _sigmoid(_dot(y.astype(BF16), w_ref[...]) + b_ref[...])).astype(o_ref.dtype)


def _s5(u, tables, d_skip, glu_w, glu_b, tm=512):
    b, t, _ = u.shape
    nc = t // S5_CHUNK
    m = nc * b
    tz, ws, wy, a1, a2 = tables
    ug = u.reshape(b, nc, S5_CHUNK, S5_GROUPS, S5_GROUP).transpose(3, 1, 0, 2, 4)
    ug = ug.reshape(S5_GROUPS, m, S5_ROW).astype(BF16)
    grp = lambda r, c: pl.BlockSpec((1, r, c), lambda g: (g, 0, 0))
    y = pl.pallas_call(
        functools.partial(_s5_kernel, nb=b, nc=nc),
        grid=(S5_GROUPS,),
        in_specs=[grp(m, S5_ROW), grp(S5_ROW, S5_ROW), grp(S5_ROW, 2 * S5_STATE),
                  grp(2 * S5_STATE, S5_ROW), grp(1, 2 * S5_STATE), grp(1, 2 * S5_STATE)],
        out_specs=grp(m, S5_ROW),
        out_shape=jax.ShapeDtypeStruct((S5_GROUPS, m, S5_ROW), F32),
        scratch_shapes=[pltpu.VMEM((m, 2 * S5_STATE), F32), pltpu.VMEM((m, 2 * S5_STATE), F32)],
        compiler_params=_params("parallel"),
        name="s5_scan",
    )(ug, tz, ws, wy, a1, a2)
    y = y.reshape(S5_GROUPS, nc, b, S5_CHUNK, S5_GROUP).transpose(2, 1, 3, 0, 4).reshape(b * t, BRANCH_WIDTH)
    tm = min(tm, b * t)
    row = pl.BlockSpec((tm, BRANCH_WIDTH), lambda i: (i, 0))
    return pl.pallas_call(
        _s5_out_kernel,
        grid=(b * t // tm,),
        in_specs=[row, row, _resident((1, BRANCH_WIDTH)), _resident((BRANCH_WIDTH, BRANCH_WIDTH)),
                  _resident((1, BRANCH_WIDTH))],
        out_specs=row,
        out_shape=jax.ShapeDtypeStruct((b * t, BRANCH_WIDTH), BF16),
        compiler_params=_params("parallel"),
        name="s5_out",
    )(y, u.reshape(b * t, BRANCH_WIDTH), d_skip.reshape(1, BRANCH_WIDTH).astype(F32), glu_w.astype(BF16),
      glu_b.reshape(1, BRANCH_WIDTH).astype(F32))


RWKV_LORA_OFF = 3 * BRANCH_WIDTH
RWKV_GATE_OFF = RWKV_LORA_OFF + LANES


def _rwkv_pre_kernel(p_ref, prev_ref, mu_ref, w0_ref, w2_ref, a0_ref, a2_ref, g2_ref, kk_ref, ka_ref,
                     r_o, w_o, k_o, v_o, kk_o, a_o, g_o, pad_scr, *, tm):
    j = pl.program_id(1)
    pad_scr[0:SUBLANES, :] = jnp.where(j == 0, 0.0, prev_ref[0])
    pad_scr[SUBLANES:SUBLANES + tm, :] = p_ref[0]
    p = p_ref[0]
    p = p + (pad_scr[SUBLANES - 1:SUBLANES - 1 + tm, :] - p) * mu_ref[...]
    r = p[:, 0:BRANCH_WIDTH]
    k = p[:, BRANCH_WIDTH:2 * BRANCH_WIDTH]
    v = p[:, 2 * BRANCH_WIDTH:3 * BRANCH_WIDTH]
    lora = p[:, RWKV_LORA_OFF:RWKV_GATE_OFF]
    dg = p[:, RWKV_GATE_OFF:RWKV_COLS]
    w = -_softplus(-(w0_ref[...] + _dot(jnp.tanh(lora).astype(BF16), w2_ref[...]))) - 0.5
    a = _sigmoid(a0_ref[...] + _dot(lora.astype(BF16), a2_ref[...]))
    r_o[0] = r
    w_o[0] = jnp.exp(-jnp.exp(w))
    k_o[0] = k * (1.0 + (a - 1.0) * ka_ref[...])
    v_o[0] = v
    kk_o[0] = k * kk_ref[...]
    a_o[0] = a
    g_o[0] = _dot(_sigmoid(dg).astype(BF16), g2_ref[...])


def _rwkv_scan_kernel(r_ref, w_ref, k_ref, v_ref, kk_ref, a_ref, rk_ref, lng_ref, lnb_ref, y_ref,
                      s_scr, av_scr, bv_scr, *, tb):
    @pl.when(pl.program_id(0) == 0)
    def _():
        s_scr[...] = jnp.zeros_like(s_scr)

    kk = kk_ref[...]
    nrm = jnp.sqrt(jnp.sum(kk * kk, axis=1, keepdims=True))
    kk = kk / jnp.maximum(nrm, 1e-12)
    av_scr[...] = -kk
    bv_scr[...] = kk * a_ref[...]

    def step(t, carry):
        sa = jnp.zeros(s_scr.shape[1:], F32)
        for kx in range(HEAD_DIM):
            sa = sa + s_scr[kx] * av_scr[t, kx:kx + 1, :]
        v_t = v_ref[t]
        y = jnp.zeros(s_scr.shape[1:], F32)
        for kx in range(HEAD_DIM):
            s_new = (s_scr[kx] * w_ref[t, kx:kx + 1, :] + sa * bv_scr[t, kx:kx + 1, :]
                     + v_t * k_ref[t, kx:kx + 1, :])
            s_scr[kx] = s_new
            y = y + s_new * r_ref[t, kx:kx + 1, :]
        y_ref[t] = y
        return carry

    lax.fori_loop(0, tb, step, 0)

    y = y_ref[...]
    mean = jnp.mean(y, axis=1, keepdims=True)
    var = jnp.mean(jnp.square(y - mean), axis=1, keepdims=True)
    y = (y - mean) * lax.rsqrt(var + RWKV_GN_EPS) * lng_ref[...] + lnb_ref[...]
    bonus = jnp.sum(r_ref[...] * k_ref[...] * rk_ref[...], axis=1, keepdims=True) * v_ref[...]
    y_ref[...] = y + bonus


def _rwkv(cols, mu, w0, w2, a0, a2, g2, k_k, k_a, r_k, ln_g, ln_b, tm=256, tb=32):
    b, t, _ = cols.shape
    tm, tb = min(tm, t), min(tb, t)
    vec = lambda p: p.reshape(1, -1).astype(F32)
    zpad = jnp.zeros((LANES // 2, BRANCH_WIDTH), F32)
    blk = lambda n: pl.BlockSpec((1, tm, n), lambda i, j: (i, j, 0))
    prev = pl.BlockSpec((1, SUBLANES, RWKV_COLS),
                        lambda i, j: (i, jnp.maximum(j * (tm // SUBLANES) - 1, 0), 0))
    outs = pl.pallas_call(
        functools.partial(_rwkv_pre_kernel, tm=tm),
        grid=(b, t // tm),
        in_specs=[blk(RWKV_COLS), prev, _resident((1, RWKV_COLS)), _resident((1, BRANCH_WIDTH)),
                  _resident((LANES, BRANCH_WIDTH)), _resident((1, BRANCH_WIDTH)),
                  _resident((LANES, BRANCH_WIDTH)), _resident((LANES, BRANCH_WIDTH)),
                  _resident((1, BRANCH_WIDTH)), _resident((1, BRANCH_WIDTH))],
        out_specs=[blk(BRANCH_WIDTH)] * 7,
        out_shape=[jax.ShapeDtypeStruct((b, t, BRANCH_WIDTH), F32)] * 7,
        scratch_shapes=[pltpu.VMEM((SUBLANES + tm, RWKV_COLS), F32)],
        compiler_params=_params("parallel", "arbitrary"),
        name="rwkv_pre",
    )(cols, cols, vec(mu), vec(w0), jnp.concatenate([w2.astype(F32), zpad]).astype(BF16), vec(a0),
      jnp.concatenate([zpad, a2.astype(F32)]).astype(BF16), g2.astype(BF16), vec(k_k), vec(k_a))
    *seqs, gate = outs
    to_scan = lambda s: s.reshape(b, t, N_HEADS, HEAD_DIM).transpose(1, 3, 0, 2).reshape(t, HEAD_DIM, b * N_HEADS)
    per_head = lambda p: jnp.tile(p.astype(F32).reshape(N_HEADS, HEAD_DIM).T, (1, b))[None]
    nl = b * N_HEADS
    tblk = pl.BlockSpec((tb, HEAD_DIM, nl), lambda i: (i, 0, 0))
    y = pl.pallas_call(
        functools.partial(_rwkv_scan_kernel, tb=tb),
        grid=(t // tb,),
        in_specs=[tblk] * 6 + [_resident((1, HEAD_DIM, nl))] * 3,
        out_specs=tblk,
        out_shape=jax.ShapeDtypeStruct((t, HEAD_DIM, nl), F32),
        scratch_shapes=[pltpu.VMEM((HEAD_DIM, HEAD_DIM, nl), F32), pltpu.VMEM((tb, HEAD_DIM, nl), F32),
                        pltpu.VMEM((tb, HEAD_DIM, nl), F32)],
        compiler_params=_params("arbitrary"),
        name="rwkv_scan",
    )(*[to_scan(s) for s in seqs], per_head(r_k), per_head(ln_g), per_head(ln_b))
    y = y.reshape(t, HEAD_DIM, b, N_HEADS).transpose(2, 0, 3, 1).reshape(b * t, BRANCH_WIDTH)
    return y, gate.reshape(b * t, BRANCH_WIDTH)


DSA_TQ = 128
DSA_TK = 512
INT_MIN = -2 ** 31
MASKED = -1e30


def _bias_tiles():
    n = np.arange(1, 8192, dtype=np.float32)
    large = 8 + (np.log(n / np.float32(8)) / np.float32(math.log(REL_MAX_DIST / 8)) * np.float32(8)).astype(np.int32)
    n_sat = int(n[np.argmax(large >= 15)])
    return -(-(n_sat + DSA_TQ - 1) // DSA_TQ) + 1


def _bias_kernel(rb_ref, o_ref):
    e = pl.program_id(0)
    t = DSA_TQ
    rel = (lax.broadcasted_iota(jnp.int32, (t, t), 1) - lax.broadcasted_iota(jnp.int32, (t, t), 0)) - e * t
    half, exact = REL_BUCKETS // 2, REL_BUCKETS // 4
    n = jnp.abs(rel)
    n_f = jnp.maximum(n, 1).astype(F32)
    large = exact + (jnp.log(n_f / exact) / math.log(REL_MAX_DIST / exact) * (half - exact)).astype(jnp.int32)
    large = jnp.minimum(large, half - 1)
    bucket = jnp.where(rel > 0, half, 0) + jnp.where(n < exact, n, large)
    for h in range(N_HEADS):
        acc = jnp.zeros((t, t), F32)
        for bkt in range(REL_BUCKETS):
            acc = jnp.where(bucket == bkt, rb_ref[bkt, h], acc)
        o_ref[h, 0] = acc


def _rel_bias_table(rel_bias):
    nd = _bias_tiles()
    t = DSA_TQ
    return pl.pallas_call(
        _bias_kernel,
        grid=(nd,),
        in_specs=[pl.BlockSpec(memory_space=pltpu.SMEM)],
        out_specs=pl.BlockSpec((N_HEADS, 1, t, t), lambda e: (0, e, 0, 0)),
        out_shape=jax.ShapeDtypeStruct((N_HEADS, nd, t, t), F32),
        compiler_params=_params("parallel"),
        name="rel_bias",
    )(rel_bias.astype(F32))


def _dsa_kernel(q_ref, k_ref, v_ref, qi_ref, kit_ref, small_ref, bias_ref, tri_ref, o_ref,
                key_scr, msk_scr, *, topk, nd):
    tq, tk = DSA_TQ, DSA_TK
    sub = tk // LANES
    q0 = pl.program_id(1) * tq
    nkb = (q0 + tq + tk - 1) // tk
    int_min = jnp.int32(INT_MIN)
    row_pos = q0 + lax.broadcasted_iota(jnp.int32, (tq, 1), 0)
    lim = (row_pos // CHUNK + 1) * CHUNK
    lane = lax.broadcasted_iota(jnp.int32, (tq, LANES), 1)

    qi = qi_ref[0]
    sm = small_ref[0]
    qi_h = [jnp.where((lane >= h * IDX_DIM) & (lane < (h + 1) * IDX_DIM), qi, 0.0).astype(BF16)
            for h in range(IDX_HEADS)]
    w_h = [sm[:, 40 + h:41 + h] for h in range(IDX_HEADS)]

    def score_block(j, carry):
        s0 = pl.multiple_of(j * tk, tk)
        kit = kit_ref[0, :, pl.ds(s0, tk)].astype(BF16)
        s = None
        for h in range(IDX_HEADS):
            term = w_h[h] * jnp.maximum(_dot(qi_h[h], kit), 0.0)
            s = term if s is None else s + term
        s = s + 0.0
        bits = lax.bitcast_convert_type(s, jnp.int32)
        bits = jnp.where(bits < 0, bits ^ jnp.int32(0x7FFFFFFF), bits)
        pos = s0 + lax.broadcasted_iota(jnp.int32, (tq, tk), 1)
        key_scr[:, pl.ds(s0, tk)] = jnp.where(pos < lim, bits, int_min)
        return carry

    lax.fori_loop(0, nkb, score_block, 0)

    def count(pred):
        def body(j, acc):
            hit = jnp.where(pred(key_scr[:, pl.ds(pl.multiple_of(j * tk, tk), tk)]), 1.0, 0.0)
            for c in range(sub):
                acc = acc + hit[:, c * LANES:(c + 1) * LANES]
            return acc
        acc = lax.fori_loop(0, nkb, body, jnp.zeros((tq, LANES), F32))
        return jnp.sum(acc, axis=1, keepdims=True)

    kf = jnp.float32(topk)
    thr = jnp.where(count(lambda kb: kb >= 0) >= kf, jnp.int32(0), int_min)

    def search(it, thr):
        cand = thr + jnp.left_shift(jnp.int32(1), 30 - it)
        return jnp.where(count(lambda kb: kb >= cand) >= kf, cand, thr)

    thr = lax.fori_loop(0, 31, search, thr)
    quota = kf - count(lambda kb: kb > thr)
    real = thr > int_min

    def select_block(j, run):
        s0 = pl.multiple_of(j * tk, tk)
        kb = key_scr[:, pl.ds(s0, tk)]
        for c in range(sub):
            kc = kb[:, c * LANES:(c + 1) * LANES]
            eq = kc == thr
            rank = run + _dot(jnp.where(eq, 1.0, 0.0).astype(BF16), tri_ref[...])
            sel = (kc > thr) | (eq & real & (rank <= quota))
            msk_scr[:, pl.ds(s0 + c * LANES, LANES)] = jnp.where(sel, 0.0, MASKED)
            run = rank[:, LANES - 1:LANES]
        return run

    lax.fori_loop(0, nkb, select_block, jnp.zeros((tq, 1), F32))

    q = q_ref[0]
    for hp in range(N_HEADS // 2):
        pair = []
        for hh in range(2):
            h = 2 * hp + hh
            in_head = (lane >= hh * HEAD_DIM) & (lane < (hh + 1) * HEAD_DIM)
            q_h = jnp.where(in_head, q[:, hp * LANES:(hp + 1) * LANES] * HEAD_DIM ** -0.5, 0.0).astype(BF16)

            def attend(j, carry, h=h, q_h=q_h):
                m, l, acc = carry
                s0 = pl.multiple_of(j * tk, tk)
                k_blk = k_ref[0, pl.ds(s0, tk), hp * LANES:(hp + 1) * LANES]
                v_blk = v_ref[0, pl.ds(s0, tk), hp * LANES:(hp + 1) * LANES]
                lg = lax.dot_general(q_h, k_blk, (((1,), (1,)), ((), ())), preferred_element_type=F32)
                lg = lg + msk_scr[:, pl.ds(s0, tk)]
                e0 = (q0 - s0) // tq
                parts = []
                for c in range(sub):
                    d = jnp.clip(e0 - c, 0, nd - 1)
                    parts.append(lg[:, c * LANES:(c + 1) * LANES] + bias_ref[h, d])
                m_new = m
                for pc in parts:
                    m_new = jnp.maximum(m_new, jnp.max(pc, axis=1, keepdims=True))
                alpha = jnp.exp(m - m_new)
                p = jnp.concatenate([jnp.exp(pc - m_new) for pc in parts], axis=1)
                l = alpha * l + jnp.sum(p, axis=1, keepdims=True)
                acc = alpha * acc + _dot(p.astype(BF16), v_blk)
                return m_new, l, acc

            m, l, acc = lax.fori_loop(
                0, nkb, attend,
                (jnp.full((tq, 1), MASKED, F32), jnp.zeros((tq, 1), F32), jnp.zeros((tq, LANES), F32)))
            pair.append(acc / l)
        o_ref[0, :, hp * LANES:(hp + 1) * LANES] = jnp.where(lane < HEAD_DIM, pair[0], pair[1]).astype(o_ref.dtype)


def _dsa(q, k, v, qi, small, bias_tab):
    b, t, _ = q.shape
    topk = min(DSA_MAX_TOPK, t // 4)
    nd = bias_tab.shape[1]
    kit = jnp.tile(small[:, :, 8:8 + IDX_DIM].transpose(0, 2, 1), (1, IDX_HEADS, 1))
    tri = jnp.asarray(np.triu(np.ones((LANES, LANES), np.float32))).astype(BF16)
    qblk = lambda n: pl.BlockSpec((1, DSA_TQ, n), lambda i, j: (i, j, 0))
    full = lambda r, c: pl.BlockSpec((1, r, c), lambda i, j: (i, 0, 0))
    return pl.pallas_call(
        functools.partial(_dsa_kernel, topk=topk, nd=nd),
        grid=(b, t // DSA_TQ),
        in_specs=[qblk(BRANCH_WIDTH), full(t, BRANCH_WIDTH), full(t, BRANCH_WIDTH), qblk(LANES),
                  full(LANES, t), qblk(LANES), _resident(bias_tab.shape), _resident((LANES, LANES))],
        out_specs=qblk(BRANCH_WIDTH),
        out_shape=jax.ShapeDtypeStruct((b, t, BRANCH_WIDTH), BF16),
        scratch_shapes=[pltpu.VMEM((DSA_TQ, t), jnp.int32), pltpu.VMEM((DSA_TQ, t), F32)],
        compiler_params=_params("parallel", "arbitrary"),
        name="dsa",
    )(q, k, v, qi, kit, small, bias_tab, tri)


def kernel(x, rel_bias, norm_mix_g, w_in, ssd_conv_w, ssd_conv_b, ssd_dt_bias, ssd_a_log, ssd_d, ssd_norm_g,
           rwkv_mu, rwkv_w0, rwkv_w2, rwkv_a0, rwkv_a2, rwkv_g2, rwkv_k_k, rwkv_k_a, rwkv_r_k, rwkv_ln_g,
           rwkv_ln_b, s5_a_re, s5_a_im, s5_b_re, s5_b_im, s5_c_re, s5_c_im, s5_d, s5_log_dt, s5_glu_w,
           s5_glu_b, w_branch, w_out, norm_ffn_g, ffn_w1, ffn_w3, ffn_w2, norm_final_g):
    b, t, _ = x.shape
    x2 = x.reshape(b * t, D_MODEL).astype(F32)
    seq = lambda a: a.reshape(b, t, a.shape[-1])
    bias_tab = _rel_bias_table(rel_bias)
    for i in range(DEPTH):
        z, xbc, q, k, v, qi, small, rwkv_cols, s5_u, gates = _proj(x2, norm_mix_g[i], _pack_w_in(w_in[i]))
        small = seq(small)
        y_a = _ssd(seq(z), seq(xbc), small, ssd_conv_w[i], ssd_conv_b[i], ssd_dt_bias[i], ssd_a_log[i],
                   ssd_d[i], ssd_norm_g[i])
        y_b = _dsa(seq(q), seq(k), seq(v), seq(qi), small, bias_tab)
        y_c, g_c = _rwkv(seq(rwkv_cols), rwkv_mu[i], rwkv_w0[i], rwkv_w2[i], rwkv_a0[i], rwkv_a2[i],
                         rwkv_g2[i], rwkv_k_k[i], rwkv_k_a[i], rwkv_r_k[i], rwkv_ln_g[i], rwkv_ln_b[i])
        tables = _s5_tables(s5_a_re[i], s5_a_im[i], s5_b_re[i], s5_b_im[i], s5_c_re[i], s5_c_im[i],
                            s5_log_dt[i])
        y_d = _s5(seq(s5_u), tables, s5_d[i], s5_glu_w[i], s5_glu_b[i])
        x2 = _merge(x2, y_a.reshape(b * t, BRANCH_WIDTH), y_b.reshape(b * t, BRANCH_WIDTH), y_c, g_c, y_d,
                    gates, w_branch[i].astype(BF16), w_out[i].astype(BF16))
        x2 = _ffn(x2, norm_ffn_g[i], ffn_w1[i].astype(BF16), ffn_w3[i].astype(BF16), ffn_w2[i].astype(BF16),
                  norm_final_g, final_norm=(i == DEPTH - 1))
    return x2.reshape(b, t, D_MODEL)
```

```python
import functools
import math

import jax
import jax.numpy as jnp
import numpy as np
from jax import lax
from jax.experimental import pallas as pl
from jax.experimental.pallas import tpu as pltpu

D_MODEL = 1024
DEPTH = 4
CHUNK = 64
N_BRANCH = 4
BRANCH_WIDTH = 512
HEAD_DIM = 64
N_HEADS = 8
RMS_EPS = 1e-6

SSD_GROUPS = 2
SSD_STATE = 64
SSD_CONV = 4
SSD_XBC = 768

IDX_HEADS = 4
IDX_DIM = 32
DSA_MAX_TOPK = 256
REL_BUCKETS = 32
REL_MAX_DIST = 256

RWKV_COLS = 1792
RWKV_GN_EPS = 64e-5

S5_GROUP = 16
S5_GROUPS = 32
S5_STATE = 64

FFN_HIDDEN = 2816

LANES = 128
SUBLANES = 8
VMEM_LIMIT = 56 * 1024 * 1024

F32 = jnp.float32
BF16 = jnp.bfloat16
HIGHEST = lax.Precision.HIGHEST


def _params(*sem):
    return pltpu.CompilerParams(dimension_semantics=sem, vmem_limit_bytes=VMEM_LIMIT)


def _resident(shape):
    nd = len(shape)
    return pl.BlockSpec(shape, lambda *_: (0,) * nd, pipeline_mode=pl.Buffered(1))


def _dot(a, b):
    return jnp.dot(a, b, preferred_element_type=F32)


def _dot_exact(a, b):
    return jnp.dot(a, b, preferred_element_type=F32, precision=HIGHEST)


def _sigmoid(x):
    return 1.0 / (1.0 + jnp.exp(-x))


def _silu(x):
    return x * _sigmoid(x)


def _softplus(x):
    return jnp.maximum(x, 0.0) + jnp.log(1.0 + jnp.exp(-jnp.abs(x)))


PROJ_SEGS = (
    ("z", 512, F32), ("xbc", 768, F32), ("q", 512, BF16), ("k", 512, BF16), ("v", 512, BF16),
    ("qi", 128, F32), ("small", 128, F32), ("kidx", 128, BF16), ("rwkv", RWKV_COLS, F32), ("s5", 512, F32),
    ("gates", 4096, F32),
)
PROJ_COLS = sum(s[1] for s in PROJ_SEGS)


def _pack_w_in(w):
    o = np.cumsum((0, 512, 768, 8, 512, 512, 512, 128, 32, 4, RWKV_COLS, 512, 4096))
    seg = lambda i: w[:, o[i]:o[i + 1]]
    small = jnp.concatenate([seg(2), seg(7), seg(8), jnp.zeros((D_MODEL, LANES - 44), w.dtype)], axis=1)
    kidx = jnp.tile(seg(7), (1, IDX_HEADS))
    parts = [seg(0), seg(1), seg(3), seg(4), seg(5), seg(6), small, kidx, seg(9), seg(10), seg(11)]
    return jnp.concatenate(parts, axis=1).astype(BF16)


def _proj_kernel(x_ref, g_ref, w_ref, *out_refs):
    x = x_ref[...]
    ms = jnp.mean(x * x, axis=-1, keepdims=True)
    h = (x * lax.rsqrt(ms + RMS_EPS) * g_ref[...]).astype(BF16)
    off = 0
    for o_ref, (_, n, _) in zip(out_refs, PROJ_SEGS):
        o_ref[...] = _dot(h, w_ref[:, off:off + n]).astype(o_ref.dtype)
        off += n


def _proj(x2, g, w_packed, tm=256):
    m = x2.shape[0]
    tm = min(tm, m)
    return pl.pallas_call(
        _proj_kernel,
        grid=(m // tm,),
        in_specs=[pl.BlockSpec((tm, D_MODEL), lambda i: (i, 0)),
                  _resident((1, D_MODEL)), _resident((D_MODEL, PROJ_COLS))],
        out_specs=[pl.BlockSpec((tm, n), lambda i: (i, 0)) for _, n, _ in PROJ_SEGS],
        out_shape=[jax.ShapeDtypeStruct((m, n), dt) for _, n, dt in PROJ_SEGS],
        compiler_params=_params("parallel"),
        name="proj",
    )(x2, g.reshape(1, D_MODEL), w_packed)


def _merge_kernel(x_ref, ya_ref, yb_ref, yc_ref, gc_ref, yd_ref, gate_ref, wb_ref, wo_ref, o_ref):
    ys = (ya_ref[...], yb_ref[...], (yc_ref[...] * gc_ref[...]).astype(BF16), yd_ref[...])
    merged = None
    for n, y in enumerate(ys):
        gate = _sigmoid(gate_ref[:, n * D_MODEL:(n + 1) * D_MODEL])
        term = gate * _dot(y, wb_ref[n])
        merged = term if merged is None else merged + term
    o_ref[...] = x_ref[...] + _dot(merged.astype(BF16), wo_ref[...])


def _merge(x2, y_a, y_b, y_c, g_c, y_d, gates, wb, wo, tm=256):
    m = x2.shape[0]
    tm = min(tm, m)
    row = lambda n: pl.BlockSpec((tm, n), lambda i: (i, 0))
    return pl.pallas_call(
        _merge_kernel,
        grid=(m // tm,),
        in_specs=[row(D_MODEL)] + [row(BRANCH_WIDTH)] * 5 + [row(N_BRANCH * D_MODEL),
                  _resident((N_BRANCH, BRANCH_WIDTH, D_MODEL)), _resident((D_MODEL, D_MODEL))],
        out_specs=row(D_MODEL),
        out_shape=jax.ShapeDtypeStruct((m, D_MODEL), F32),
        compiler_params=_params("parallel"),
        name="merge",
    )(x2, y_a, y_b, y_c, g_c, y_d, gates, wb, wo)


def _ffn_kernel(x_ref, g_ref, w1_ref, w3_ref, w2_ref, gf_ref, o_ref, *, final_norm):
    x = x_ref[...]
    ms = jnp.mean(x * x, axis=-1, keepdims=True)
    h = (x * lax.rsqrt(ms + RMS_EPS) * g_ref[...]).astype(BF16)
    hid = _silu(_dot(h, w1_ref[...])) * _dot(h, w3_ref[...])
    x = x + _dot(hid.astype(BF16), w2_ref[...])
    if final_norm:
        ms = jnp.mean(x * x, axis=-1, keepdims=True)
        x = x * lax.rsqrt(ms + RMS_EPS) * gf_ref[...]
    o_ref[...] = x


def _ffn(x2, g, w1, w3, w2, g_final, final_norm, tm=256):
    m = x2.shape[0]
    tm = min(tm, m)
    row = pl.BlockSpec((tm, D_MODEL), lambda i: (i, 0))
    return pl.pallas_call(
        functools.partial(_ffn_kernel, final_norm=final_norm),
        grid=(m // tm,),
        in_specs=[row, _resident((1, D_MODEL)), _resident((D_MODEL, FFN_HIDDEN)),
                  _resident((D_MODEL, FFN_HIDDEN)), _resident((FFN_HIDDEN, D_MODEL)),
                  _resident((1, D_MODEL))],
        out_specs=row,
        out_shape=jax.ShapeDtypeStruct((m, D_MODEL), F32),
        compiler_params=_params("parallel"),
        name="ffn",
    )(x2, g.reshape(1, D_MODEL), w1, w3, w2, g_final.reshape(1, D_MODEL))


def _head_expand():
    e = np.zeros((LANES, BRANCH_WIDTH), np.float32)
    for h in range(N_HEADS):
        e[h, h * HEAD_DIM:(h + 1) * HEAD_DIM] = 1.0
    return jnp.asarray(e)


def _head_select():
    s = np.zeros((N_HEADS, BRANCH_WIDTH), np.float32)
    for h in range(N_HEADS):
        s[h, h * HEAD_DIM] = 1.0
    return jnp.asarray(s)


def _ssd_kernel(z_ref, xbc_ref, xprev_ref, small_ref, cw_ref, cb_ref, dtb_ref, alog_ref, dsk_ref,
                ng_ref, exp_ref, sel_ref, o_ref, xpad_scr, state_scr, y_scr, *, L):
    c = pl.program_id(1)

    @pl.when(c == 0)
    def _():
        state_scr[...] = jnp.zeros_like(state_scr)

    xpad_scr[0:SUBLANES, :] = jnp.where(c == 0, 0.0, xprev_ref[0])
    xpad_scr[SUBLANES:SUBLANES + L, :] = xbc_ref[0]
    acc = cb_ref[...]
    for j in range(SSD_CONV):
        lo = SUBLANES - (SSD_CONV - 1) + j
        acc = acc + cw_ref[j:j + 1, :] * xpad_scr[lo:lo + L, :]
    xc = _silu(acc)
    xs = xc[:, :BRANCH_WIDTH]

    dt = _softplus(_dot_exact(small_ref[0], exp_ref[...]) + dtb_ref[...])
    a = dt * (-jnp.exp(alog_ref[...]))
    rows = lax.broadcasted_iota(jnp.int32, (L, L), 0)
    cols = lax.broadcasted_iota(jnp.int32, (L, L), 1)
    causal = rows >= cols
    acs = _dot_exact(causal.astype(F32), a)
    acs_t = lax.dot_general(sel_ref[...], acs, (((1,), (1,)), ((), ())),
                            precision=HIGHEST, preferred_element_type=F32)
    xdt = xs * dt

    for g in range(SSD_GROUPS):
        b_g = xc[:, BRANCH_WIDTH + g * SSD_STATE:BRANCH_WIDTH + (g + 1) * SSD_STATE].astype(BF16)
        c_g = xc[:, BRANCH_WIDTH + (SSD_GROUPS + g) * SSD_STATE:
                 BRANCH_WIDTH + (SSD_GROUPS + g + 1) * SSD_STATE].astype(BF16)
        cb = lax.dot_general(c_g, b_g, (((1,), (1,)), ((), ())), preferred_element_type=F32)
        for hh in range(N_HEADS // SSD_GROUPS):
            h = g * (N_HEADS // SSD_GROUPS) + hh
            sl = slice(h * HEAD_DIM, (h + 1) * HEAD_DIM)
            col = acs[:, h * HEAD_DIM:h * HEAD_DIM + 1]
            row = acs_t[h:h + 1, :]
            dec = jnp.where(causal, jnp.exp(jnp.minimum(col - row, 0.0)), 0.0)
            x_h = xdt[:, sl]
            st = state_scr[h]
            y_h = _dot((cb * dec).astype(BF16), x_h.astype(BF16))
            y_h = y_h + _dot(c_g, st.astype(BF16)) * jnp.exp(col)
            y_scr[:, sl] = y_h
            last = acs[L - 1:L, h * HEAD_DIM:h * HEAD_DIM + 1]
            x_w = (x_h * jnp.exp(last - col)).astype(BF16)
            state_scr[h] = st * jnp.exp(last) + lax.dot_general(
                b_g, x_w, (((0,), (0,)), ((), ())), preferred_element_type=F32)

    y = y_scr[...] + dsk_ref[...] * xs
    y = y * _silu(z_ref[0])
    half = BRANCH_WIDTH // SSD_GROUPS
    for g in range(SSD_GROUPS):
        yg = y[:, g * half:(g + 1) * half]
        yg = yg * lax.rsqrt(jnp.mean(yg * yg, axis=-1, keepdims=True) + RMS_EPS)
        o_ref[0, :, g * half:(g + 1) * half] = (yg * ng_ref[:, g * half:(g + 1) * half]).astype(o_ref.dtype)


def _ssd(z, xbc, small, conv_w, conv_b, dt_bias, a_log, d_skip, norm_g, L=64):
    b, t, _ = z.shape
    rep = lambda p: jnp.repeat(p.astype(F32), HEAD_DIM).reshape(1, BRANCH_WIDTH)
    blk = lambda n: pl.BlockSpec((1, L, n), lambda i, j: (i, j, 0))
    prev = pl.BlockSpec((1, SUBLANES, SSD_XBC),
                        lambda i, j: (i, jnp.maximum(j * (L // SUBLANES) - 1, 0), 0))
    return pl.pallas_call(
        functools.partial(_ssd_kernel, L=L),
        grid=(b, t // L),
        in_specs=[blk(BRANCH_WIDTH), blk(SSD_XBC), prev, blk(LANES),
                  _resident((SSD_CONV, SSD_XBC)), _resident((1, SSD_XBC)),
                  _resident((1, BRANCH_WIDTH)), _resident((1, BRANCH_WIDTH)), _resident((1, BRANCH_WIDTH)),
                  _resident((1, BRANCH_WIDTH)), _resident((LANES, BRANCH_WIDTH)),
                  _resident((N_HEADS, BRANCH_WIDTH))],
        out_specs=blk(BRANCH_WIDTH),
        out_shape=jax.ShapeDtypeStruct((b, t, BRANCH_WIDTH), BF16),
        scratch_shapes=[pltpu.VMEM((SUBLANES + L, SSD_XBC), F32),
                        pltpu.VMEM((N_HEADS, SSD_STATE, HEAD_DIM), F32),
                        pltpu.VMEM((L, BRANCH_WIDTH), F32)],
        compiler_params=_params("parallel", "arbitrary"),
        name="ssd",
    )(z, xbc, xbc, small, conv_w, conv_b.reshape(1, SSD_XBC), rep(dt_bias), rep(a_log), rep(d_skip),
      norm_g.reshape(1, BRANCH_WIDTH), _head_expand(), _head_select())


S5_CHUNK = 64
S5_ROW = S5_CHUNK * S5_GROUP


def _s5_tables(a_re, a_im, b_re, b_im, c_re, c_im, log_dt):
    L = S5_CHUNK
    f = lambda v: v.astype(F32)
    a_re, a_im, b_re, b_im, c_re, c_im = map(f, (a_re, a_im, b_re, b_im, c_re, c_im))
    dt = jnp.exp(f(log_dt))[:, None]
    lr, li = a_re * dt, a_im * dt
    tau = jnp.arange(L + 1, dtype=F32)[None, :, None]
    mag = jnp.exp(lr[:, None, :] * tau)
    pr, pi = mag * jnp.cos(li[:, None, :] * tau), mag * jnp.sin(li[:, None, :] * tau)
    abr, abi = pr[:, 1], pi[:, 1]
    den = a_re * a_re + a_im * a_im
    qr = ((abr - 1.0) * a_re + abi * a_im) / den
    qi = (abi * a_re - (abr - 1.0) * a_im) / den
    bbr = qr[..., None] * b_re - qi[..., None] * b_im
    bbi = qr[..., None] * b_im + qi[..., None] * b_re
    pbr = pr[..., None] * bbr[:, None] - pi[..., None] * bbi[:, None]
    pbi = pr[..., None] * bbi[:, None] + pi[..., None] * bbr[:, None]
    kern = (jnp.einsum("gop,gtpi->gtoi", c_re, pbr[:, :L], precision=HIGHEST)
            - jnp.einsum("gop,gtpi->gtoi", c_im, pbi[:, :L], precision=HIGHEST))
    lag = jnp.arange(L)[None, :] - jnp.arange(L)[:, None]
    tz = jnp.where((lag >= 0)[None, :, :, None, None], kern[:, jnp.maximum(lag, 0)], 0.0)
    tz = tz.transpose(0, 1, 4, 2, 3).reshape(S5_GROUPS, S5_ROW, S5_ROW)
    rev = L - 1 - jnp.arange(L)
    ws = jnp.concatenate([pbr[:, rev], pbi[:, rev]], axis=2)
    ws = ws.transpose(0, 1, 3, 2).reshape(S5_GROUPS, S5_ROW, 2 * S5_STATE)
    cpr = c_re[:, None] * pr[:, 1:, None, :] - c_im[:, None] * pi[:, 1:, None, :]
    cpi = c_re[:, None] * pi[:, 1:, None, :] + c_im[:, None] * pr[:, 1:, None, :]
    wy = jnp.concatenate([cpr, -cpi], axis=3).transpose(0, 3, 1, 2).reshape(S5_GROUPS, 2 * S5_STATE, S5_ROW)
    a1 = jnp.concatenate([pr[:, L], pr[:, L]], axis=1)[:, None, :]
    a2 = jnp.concatenate([-pi[:, L], pi[:, L]], axis=1)[:, None, :]
    return tz.astype(BF16), ws.astype(BF16), wy.astype(BF16), a1, a2


def _s5_kernel(u_ref, tz_ref, ws_ref, wy_ref, a1_ref, a2_ref, y_ref, sl_scr, x_scr, *, nb, nc):
    u = u_ref[0]
    y_local = _dot(u, tz_ref[0])
    sl_scr[...] = _dot(u, ws_ref[0])
    a1, a2 = a1_ref[0], a2_ref[0]

    def step(c, x):
        r = pl.ds(pl.multiple_of(c * nb, SUBLANES), nb)
        x_scr[r, :] = x
        return a1 * x + a2 * pltpu.roll(x, S5_STATE, axis=1) + sl_scr[r, :]

    lax.fori_loop(0, nc, step, jnp.zeros((nb, 2 * S5_STATE), F32))
    y_ref[0] = y_local + _dot(x_scr[...].astype(BF16), wy_ref[0])


def _s5_out_kernel(y_ref, u_ref, d_ref, w_ref, b_ref, o_ref):
    y = y_ref[...] + d_ref[...] * u_ref[...]
    y = 0.5 * y * (1.0 + jnp.tanh(math.sqrt(2.0 / math.pi) * (y + 0.044715 * (y * y * y))))
    o_ref[...] = (y * _sigmoid(_dot(y.astype(BF16), w_ref[...]) + b_ref[...])).astype(o_ref.dtype)


def _s5(u, tables, d_skip, glu_w, glu_b, tm=512):
    b, t, _ = u.shape
    nc = t // S5_CHUNK
    m = nc * b
    tz, ws, wy, a1, a2 = tables
    ug = u.reshape(b, nc, S5_CHUNK, S5_GROUPS, S5_GROUP).transpose(3, 1, 0, 2, 4)
    ug = ug.reshape(S5_GROUPS, m, S5_ROW).astype(BF16)
    grp = lambda r, c: pl.BlockSpec((1, r, c), lambda g: (g, 0, 0))
    y = pl.pallas_call(
        functools.partial(_s5_kernel, nb=b, nc=nc),
        grid=(S5_GROUPS,),
        in_specs=[grp(m, S5_ROW), grp(S5_ROW, S5_ROW), grp(S5_ROW, 2 * S5_STATE),
                  grp(2 * S5_STATE, S5_ROW), grp(1, 2 * S5_STATE), grp(1, 2 * S5_STATE)],
        out_specs=grp(m, S5_ROW),
        out_shape=jax.ShapeDtypeStruct((S5_GROUPS, m, S5_ROW), F32),
        scratch_shapes=[pltpu.VMEM((m, 2 * S5_STATE), F32), pltpu.VMEM((m, 2 * S5_STATE), F32)],
        compiler_params=_params("parallel"),
        name="s5_scan",
    )(ug, tz, ws, wy, a1, a2)
    y = y.reshape(S5_GROUPS, nc, b, S5_CHUNK, S5_GROUP).transpose(2, 1, 3, 0, 4).reshape(b * t, BRANCH_WIDTH)
    tm = min(tm, b * t)
    row = pl.BlockSpec((tm, BRANCH_WIDTH), lambda i: (i, 0))
    return pl.pallas_call(
        _s5_out_kernel,
        grid=(b * t // tm,),
        in_specs=[row, row, _resident((1, BRANCH_WIDTH)), _resident((BRANCH_WIDTH, BRANCH_WIDTH)),
                  _resident((1, BRANCH_WIDTH))],
        out_specs=row,
        out_shape=jax.ShapeDtypeStruct((b * t, BRANCH_WIDTH), BF16),
        compiler_params=_params("parallel"),
        name="s5_out",
    )(y, u.reshape(b * t, BRANCH_WIDTH), d_skip.reshape(1, BRANCH_WIDTH).astype(F32), glu_w.astype(BF16),
      glu_b.reshape(1, BRANCH_WIDTH).astype(F32))


RWKV_LORA_OFF = 3 * BRANCH_WIDTH
RWKV_GATE_OFF = RWKV_LORA_OFF + LANES


def _rwkv_pre_kernel(p_ref, prev_ref, mu_ref, w0_ref, w2_ref, a0_ref, a2_ref, g2_ref, kk_ref, ka_ref,
                     r_o, w_o, k_o, v_o, kk_o, a_o, g_o, pad_scr, *, tm):
    j = pl.program_id(1)
    pad_scr[0:SUBLANES, :] = jnp.where(j == 0, 0.0, prev_ref[0])
    pad_scr[SUBLANES:SUBLANES + tm, :] = p_ref[0]
    p = p_ref[0]
    p = p + (pad_scr[SUBLANES - 1:SUBLANES - 1 + tm, :] - p) * mu_ref[...]
    r = p[:, 0:BRANCH_WIDTH]
    k = p[:, BRANCH_WIDTH:2 * BRANCH_WIDTH]
    v = p[:, 2 * BRANCH_WIDTH:3 * BRANCH_WIDTH]
    lora = p[:, RWKV_LORA_OFF:RWKV_GATE_OFF]
    dg = p[:, RWKV_GATE_OFF:RWKV_COLS]
    w = -_softplus(-(w0_ref[...] + _dot(jnp.tanh(lora).astype(BF16), w2_ref[...]))) - 0.5
    a = _sigmoid(a0_ref[...] + _dot(lora.astype(BF16), a2_ref[...]))
    r_o[0] = r
    w_o[0] = jnp.exp(-jnp.exp(w))
    k_o[0] = k * (1.0 + (a - 1.0) * ka_ref[...])
    v_o[0] = v
    kk_o[0] = k * kk_ref[...]
    a_o[0] = a
    g_o[0] = _dot(_sigmoid(dg).astype(BF16), g2_ref[...])


def _rwkv_scan_kernel(r_ref, w_ref, k_ref, v_ref, kk_ref, a_ref, rk_ref, lng_ref, lnb_ref, y_ref,
                      s_scr, av_scr, bv_scr, *, tb):
    @pl.when(pl.program_id(0) == 0)
    def _():
        s_scr[...] = jnp.zeros_like(s_scr)

    kk = kk_ref[...]
    nrm = jnp.sqrt(jnp.sum(kk * kk, axis=1, keepdims=True))
    kk = kk / jnp.maximum(nrm, 1e-12)
    av_scr[...] = -kk
    bv_scr[...] = kk * a_ref[...]

    def step(t, carry):
        sa = jnp.zeros(s_scr.shape[1:], F32)
        for kx in range(HEAD_DIM):
            sa = sa + s_scr[kx] * av_scr[t, kx:kx + 1, :]
        v_t = v_ref[t]
        y = jnp.zeros(s_scr.shape[1:], F32)
        for kx in range(HEAD_DIM):
            s_new = (s_scr[kx] * w_ref[t, kx:kx + 1, :] + sa * bv_scr[t, kx:kx + 1, :]
                     + v_t * k_ref[t, kx:kx + 1, :])
            s_scr[kx] = s_new
            y = y + s_new * r_ref[t, kx:kx + 1, :]
        y_ref[t] = y
        return carry

    lax.fori_loop(0, tb, step, 0)

    y = y_ref[...]
    mean = jnp.mean(y, axis=1, keepdims=True)
    var = jnp.mean(jnp.square(y - mean), axis=1, keepdims=True)
    y = (y - mean) * lax.rsqrt(var + RWKV_GN_EPS) * lng_ref[...] + lnb_ref[...]
    bonus = jnp.sum(r_ref[...] * k_ref[...] * rk_ref[...], axis=1, keepdims=True) * v_ref[...]
    y_ref[...] = y + bonus


def _rwkv(cols, mu, w0, w2, a0, a2, g2, k_k, k_a, r_k, ln_g, ln_b, tm=256, tb=32):
    b, t, _ = cols.shape
    tm, tb = min(tm, t), min(tb, t)
    vec = lambda p: p.reshape(1, -1).astype(F32)
    zpad = jnp.zeros((LANES // 2, BRANCH_WIDTH), F32)
    blk = lambda n: pl.BlockSpec((1, tm, n), lambda i, j: (i, j, 0))
    prev = pl.BlockSpec((1, SUBLANES, RWKV_COLS),
                        lambda i, j: (i, jnp.maximum(j * (tm // SUBLANES) - 1, 0), 0))
    outs = pl.pallas_call(
        functools.partial(_rwkv_pre_kernel, tm=tm),
        grid=(b, t // tm),
        in_specs=[blk(RWKV_COLS), prev, _resident((1, RWKV_COLS)), _resident((1, BRANCH_WIDTH)),
                  _resident((LANES, BRANCH_WIDTH)), _resident((1, BRANCH_WIDTH)),
                  _resident((LANES, BRANCH_WIDTH)), _resident((LANES, BRANCH_WIDTH)),
                  _resident((1, BRANCH_WIDTH)), _resident((1, BRANCH_WIDTH))],
        out_specs=[blk(BRANCH_WIDTH)] * 7,
        out_shape=[jax.ShapeDtypeStruct((b, t, BRANCH_WIDTH), F32)] * 7,
        scratch_shapes=[pltpu.VMEM((SUBLANES + tm, RWKV_COLS), F32)],
        compiler_params=_params("parallel", "arbitrary"),
        name="rwkv_pre",
    )(cols, cols, vec(mu), vec(w0), jnp.concatenate([w2.astype(F32), zpad]).astype(BF16), vec(a0),
      jnp.concatenate([zpad, a2.astype(F32)]).astype(BF16), g2.astype(BF16), vec(k_k), vec(k_a))
    *seqs, gate = outs
    to_scan = lambda s: s.reshape(b, t, N_HEADS, HEAD_DIM).transpose(1, 3, 0, 2).reshape(t, HEAD_DIM, b * N_HEADS)
    per_head = lambda p: jnp.tile(p.astype(F32).reshape(N_HEADS, HEAD_DIM).T, (1, b))[None]
    nl = b * N_HEADS
    tblk = pl.BlockSpec((tb, HEAD_DIM, nl), lambda i: (i, 0, 0))
    y = pl.pallas_call(
        functools.partial(_rwkv_scan_kernel, tb=tb),
        grid=(t // tb,),
        in_specs=[tblk] * 6 + [_resident((1, HEAD_DIM, nl))] * 3,
        out_specs=tblk,
        out_shape=jax.ShapeDtypeStruct((t, HEAD_DIM, nl), F32),
        scratch_shapes=[pltpu.VMEM((HEAD_DIM, HEAD_DIM, nl), F32), pltpu.VMEM((tb, HEAD_DIM, nl), F32),
                        pltpu.VMEM((tb, HEAD_DIM, nl), F32)],
        compiler_params=_params("arbitrary"),
        name="rwkv_scan",
    )(*[to_scan(s) for s in seqs], per_head(r_k), per_head(ln_g), per_head(ln_b))
    y = y.reshape(t, HEAD_DIM, b, N_HEADS).transpose(2, 0, 3, 1).reshape(b * t, BRANCH_WIDTH)
    return y, gate.reshape(b * t, BRANCH_WIDTH)


DSA_TQ = 128
DSA_TK = 512
INT_MIN = -2 ** 31
MASKED = -1e30


def _bias_tiles():
    n = np.arange(1, 8192, dtype=np.float32)
    large = 8 + (np.log(n / np.float32(8)) / np.float32(math.log(REL_MAX_DIST / 8)) * np.float32(8)).astype(np.int32)
    n_sat = int(n[np.argmax(large >= 15)])
    return -(-(n_sat + DSA_TQ - 1) // DSA_TQ) + 1


def _bias_kernel(rb_ref, o_ref):
    e = pl.program_id(0)
    t = DSA_TQ
    rel = (lax.broadcasted_iota(jnp.int32, (t, t), 0) - lax.broadcasted_iota(jnp.int32, (t, t), 1)) - e * t
    half, exact = REL_BUCKETS // 2, REL_BUCKETS // 4
    n = jnp.abs(rel)
    n_f = jnp.maximum(n, 1).astype(F32)
    large = exact + (jnp.log(n_f / exact) / math.log(REL_MAX_DIST / exact) * (half - exact)).astype(jnp.int32)
    large = jnp.minimum(large, half - 1)
    bucket = jnp.where(rel > 0, half, 0) + jnp.where(n < exact, n, large)
    for h in range(N_HEADS):
        acc = jnp.zeros((t, t), F32)
        for bkt in range(REL_BUCKETS):
            acc = jnp.where(bucket == bkt, rb_ref[bkt, h], acc)
        o_ref[h // 2, 0, :, (h % 2) * t:(h % 2 + 1) * t] = acc


def _rel_bias_table(rel_bias):
    nd = _bias_tiles()
    t = DSA_TQ
    return pl.pallas_call(
        _bias_kernel,
        grid=(nd,),
        in_specs=[pl.BlockSpec(memory_space=pltpu.SMEM)],
        out_specs=pl.BlockSpec((N_HEADS // 2, 1, t, 2 * t), lambda e: (0, e, 0, 0)),
        out_shape=jax.ShapeDtypeStruct((N_HEADS // 2, nd, t, 2 * t), F32),
        compiler_params=_params("parallel"),
        name="rel_bias",
    )(rel_bias.astype(F32))


def _nt_dot(a, b, **kw):
    return lax.dot_general(a, b, (((1,), (1,)), ((), ())), preferred_element_type=F32, **kw)


def _dsa_kernel(q_ref, k_ref, vt_ref, qi_ref, kidx_ref, small_ref, bias_ref, tri_ref, pick_ref, o_ref,
                key_scr, msk_scr, lga_scr, lgb_scr, *, topk, nd):
    tq, tk = DSA_TQ, DSA_TK
    th = tk // 2
    q0 = pl.program_id(1) * tq
    nkb = (q0 + tq + tk - 1) // tk
    int_min = jnp.int32(INT_MIN)
    q_pos = q0 + lax.broadcasted_iota(jnp.int32, (1, tq), 1)
    lim = (q_pos // CHUNK + 1) * CHUNK
    lane = lax.broadcasted_iota(jnp.int32, (tq, LANES), 1)

    qi = qi_ref[0]
    qi_all = jnp.concatenate(
        [jnp.where((lane >= h * IDX_DIM) & (lane < (h + 1) * IDX_DIM), qi, 0.0).astype(BF16)
         for h in range(IDX_HEADS)], axis=0)
    w_t = _nt_dot(pick_ref[...], small_ref[0], precision=HIGHEST)
    w_h = [w_t[h:h + 1, :] for h in range(IDX_HEADS)]

    def score_block(j, carry):
        s0 = pl.multiple_of(j * tk, tk)
        kidx = kidx_ref[0, pl.ds(s0, tk), :]
        dots = _nt_dot(kidx, qi_all)
        s = None
        for h in range(IDX_HEADS):
            term = w_h[h] * jnp.maximum(dots[:, h * tq:(h + 1) * tq], 0.0)
            s = term if s is None else s + term
        s = s + 0.0
        bits = lax.bitcast_convert_type(s, jnp.int32)
        bits = jnp.where(bits < 0, bits ^ jnp.int32(0x7FFFFFFF), bits)
        pos = s0 + lax.broadcasted_iota(jnp.int32, (tk, tq), 0)
        key_scr[pl.ds(s0, tk), :] = jnp.where(pos < lim, bits, int_min)
        return carry

    lax.fori_loop(0, nkb, score_block, 0)

    acc_rows = 4 * SUBLANES

    def count(pred):
        def body(j, acc):
            hit = jnp.where(pred(key_scr[pl.ds(pl.multiple_of(j * tk, tk), tk), :]), 1.0, 0.0)
            return acc + jnp.sum(hit.reshape(tk // acc_rows, acc_rows, tq), axis=0)
        acc = lax.fori_loop(0, nkb, body, jnp.zeros((acc_rows, tq), F32))
        return jnp.sum(acc, axis=0, keepdims=True)

    kf = jnp.float32(topk)
    c0 = count(lambda kb: kb >= 0)
    thr = jnp.where(c0 >= kf, jnp.int32(0), int_min)
    n_ge = jnp.where(c0 >= kf, c0, 0.0)

    def search(it, carry):
        thr, n_ge = carry
        cand = thr + jnp.left_shift(jnp.int32(1), 30 - it)
        c = count(lambda kb: kb >= cand)
        return jnp.where(c >= kf, cand, thr), jnp.where(c >= kf, c, n_ge)

    thr, n_ge = lax.fori_loop(0, 31, search, (thr, n_ge))
    real = thr > int_min
    surplus = jnp.max(jnp.where(real, n_ge - kf, 0.0))

    def select_plain(j, carry):
        s0 = pl.multiple_of(j * tk, tk)
        kb = key_scr[pl.ds(s0, tk), :]
        msk_scr[pl.ds(s0, tk), :] = jnp.where((kb >= thr) & (kb > int_min), 0.0, MASKED)
        return carry

    @pl.when(surplus <= 0.0)
    def _():
        lax.fori_loop(0, nkb, select_plain, 0)

    @pl.when(surplus > 0.0)
    def _():
        quota = kf - count(lambda kb: kb > thr)

        def select_ranked(j, run):
            s0 = pl.multiple_of(j * tk, tk)
            kb = key_scr[pl.ds(s0, tk), :]
            eq = kb == thr
            rank = run + _dot(tri_ref[...], jnp.where(eq, 1.0, 0.0).astype(BF16))
            sel = (kb > thr) | (eq & real & (rank <= quota))
            msk_scr[pl.ds(s0, tk), :] = jnp.where(sel, 0.0, MASKED)
            return rank[tk - 1:tk, :]

        lax.fori_loop(0, nkb, select_ranked, jnp.zeros((1, tq), F32))

    q = q_ref[0]
    dim = lax.broadcasted_iota(jnp.int32, (LANES, tq), 0)
    for hp in range(N_HEADS // 2):
        q_pair = q[:, hp * LANES:(hp + 1) * LANES] * HEAD_DIM ** -0.5
        q2 = jnp.concatenate([jnp.where(lane < HEAD_DIM, q_pair, 0.0).astype(BF16),
                              jnp.where(lane >= HEAD_DIM, q_pair, 0.0).astype(BF16)], axis=0)

        def logits(jh, buf, hp=hp, q2=q2):
            s0 = pl.multiple_of(jh * th, th)
            e0 = (q0 - s0) // tq
            lg = _nt_dot(k_ref[0, pl.ds(s0, th), hp * LANES:(hp + 1) * LANES], q2)
            top = None
            for c in range(th // LANES):
                mask = msk_scr[pl.ds(s0 + c * LANES, LANES), :]
                lg_c = (lg[c * LANES:(c + 1) * LANES] + bias_ref[hp, jnp.clip(e0 - c, 0, nd - 1)]
                        + jnp.concatenate([mask, mask], axis=1))
                buf[c * LANES:(c + 1) * LANES, :] = lg_c
                top_c = jnp.max(lg_c, axis=0, keepdims=True)
                top = top_c if top is None else jnp.maximum(top, top_c)
            return top

        def softmax_pv(state, buf, top, j, half, hp=hp):
            m, l, acc = state
            m_new = jnp.maximum(m, top)
            alpha = jnp.exp(m - m_new)
            p = jnp.exp(buf[...] - m_new)
            l = alpha * l + jnp.sum(p, axis=0, keepdims=True)
            vt_blk = vt_ref[0, j, hp * LANES:(hp + 1) * LANES, half * th:(half + 1) * th]
            return m_new, l, alpha * acc + _dot(vt_blk, p.astype(BF16))

        def attend(j, carry):
            *state, top_a = carry
            top_b = logits(2 * j + 1, lgb_scr)
            state = softmax_pv(state, lga_scr, top_a, j, 0)
            top_a = logits(jnp.minimum(2 * j + 2, 2 * nkb - 1), lga_scr)
            state = softmax_pv(state, lgb_scr, top_b, j, 1)
            return (*state, top_a)

        init = (jnp.full((1, 2 * tq), MASKED, F32), jnp.zeros((1, 2 * tq), F32),
                jnp.zeros((LANES, 2 * tq), F32), logits(0, lga_scr))
        _, l, acc, _ = lax.fori_loop(0, nkb, attend, init)
        o2 = acc / l
        out_t = jnp.where(dim < HEAD_DIM, o2[:, :tq], o2[:, tq:])
        o_ref[0, :, hp * LANES:(hp + 1) * LANES] = out_t.T.astype(o_ref.dtype)


def _dsa(q, k, v, qi, kidx, small, bias_tab):
    b, t, _ = q.shape
    topk = min(DSA_MAX_TOPK, t // 4)
    nd = bias_tab.shape[1]
    nblk = t // DSA_TK
    vt = v.reshape(b, nblk, DSA_TK, BRANCH_WIDTH).transpose(0, 1, 3, 2)
    tri = jnp.asarray(np.tril(np.ones((DSA_TK, DSA_TK), np.float32))).astype(BF16)
    pick = np.zeros((SUBLANES, LANES), np.float32)
    pick[np.arange(IDX_HEADS), 40 + np.arange(IDX_HEADS)] = 1.0
    qblk = lambda n: pl.BlockSpec((1, DSA_TQ, n), lambda i, j: (i, j, 0))
    full = lambda *s: pl.BlockSpec((1,) + s, lambda i, j: (i,) + (0,) * len(s))
    return pl.pallas_call(
        functools.partial(_dsa_kernel, topk=topk, nd=nd),
        grid=(b, t // DSA_TQ),
        in_specs=[qblk(BRANCH_WIDTH), full(t, BRANCH_WIDTH), full(nblk, BRANCH_WIDTH, DSA_TK), qblk(LANES),
                  full(t, LANES), qblk(LANES), _resident(bias_tab.shape), _resident((DSA_TK, DSA_TK)),
                  _resident((SUBLANES, LANES))],
        out_specs=qblk(BRANCH_WIDTH),
        out_shape=jax.ShapeDtypeStruct((b, t, BRANCH_WIDTH), BF16),
        scratch_shapes=[pltpu.VMEM((t, DSA_TQ), jnp.int32), pltpu.VMEM((t, DSA_TQ), F32),
                        pltpu.VMEM((DSA_TK // 2, 2 * DSA_TQ), F32), pltpu.VMEM((DSA_TK // 2, 2 * DSA_TQ), F32)],
        compiler_params=_params("parallel", "arbitrary"),
        name="dsa",
    )(q, k, vt, qi, kidx, small, bias_tab, tri, jnp.asarray(pick))


def kernel(x, rel_bias, norm_mix_g, w_in, ssd_conv_w, ssd_conv_b, ssd_dt_bias, ssd_a_log, ssd_d, ssd_norm_g,
           rwkv_mu, rwkv_w0, rwkv_w2, rwkv_a0, rwkv_a2, rwkv_g2, rwkv_k_k, rwkv_k_a, rwkv_r_k, rwkv_ln_g,
           rwkv_ln_b, s5_a_re, s5_a_im, s5_b_re, s5_b_im, s5_c_re, s5_c_im, s5_d, s5_log_dt, s5_glu_w,
           s5_glu_b, w_branch, w_out, norm_ffn_g, ffn_w1, ffn_w3, ffn_w2, norm_final_g):
    b, t, _ = x.shape
    x2 = x.reshape(b * t, D_MODEL).astype(F32)
    seq = lambda a: a.reshape(b, t, a.shape[-1])
    bias_tab = _rel_bias_table(rel_bias)
    for i in range(DEPTH):
        z, xbc, q, k, v, qi, small, kidx, rwkv_cols, s5_u, gates = _proj(x2, norm_mix_g[i], _pack_w_in(w_in[i]))
        small = seq(small)
        y_a = _ssd(seq(z), seq(xbc), small, ssd_conv_w[i], ssd_conv_b[i], ssd_dt_bias[i], ssd_a_log[i],
                   ssd_d[i], ssd_norm_g[i])
        y_b = _dsa(seq(q), seq(k), seq(v), seq(qi), seq(kidx), small, bias_tab)
        y_c, g_c = _rwkv(seq(rwkv_cols), rwkv_mu[i], rwkv_w0[i], rwkv_w2[i], rwkv_a0[i], rwkv_a2[i],
                         rwkv_g2[i], rwkv_k_k[i], rwkv_k_a[i], rwkv_r_k[i], rwkv_ln_g[i], rwkv_ln_b[i])
        tables = _s5_tables(s5_a_re[i], s5_a_im[i], s5_b_re[i], s5_b_im[i], s5_c_re[i], s5_c_im[i],
                            s5_log_dt[i])
        y_d = _s5(seq(s5_u), tables, s5_d[i], s5_glu_w[i], s5_glu_b[i])
        x2 = _merge(x2, y_a.reshape(b * t, BRANCH_WIDTH), y_b.reshape(b * t, BRANCH_WIDTH), y_c, g_c, y_d,
                    gates, w_branch[i].astype(BF16), w_out[i].astype(BF16))
        x2 = _ffn(x2, norm_ffn_g[i], ffn_w1[i].astype(BF16), ffn_w3[i].astype(BF16), ffn_w2[i].astype(BF16),
                  norm_final_g, final_norm=(i == DEPTH - 1))
    return x2.reshape(b, t, D_MODEL)
```

```python
import functools
import math

import jax
import jax.numpy as jnp
import numpy as np
from jax import lax
from jax.experimental import pallas as pl
from jax.experimental.pallas import tpu as pltpu

D_MODEL = 1024
DEPTH = 4
CHUNK = 64
N_BRANCH = 4
BRANCH_WIDTH = 512
HEAD_DIM = 64
N_HEADS = 8
RMS_EPS = 1e-6

SSD_GROUPS = 2
SSD_STATE = 64
SSD_CONV = 4
SSD_XBC = 768

IDX_HEADS = 4
IDX_DIM = 32
DSA_MAX_TOPK = 256
REL_BUCKETS = 32
REL_MAX_DIST = 256

RWKV_COLS = 1792
RWKV_GN_EPS = 64e-5

S5_GROUP = 16
S5_GROUPS = 32
S5_STATE = 64

FFN_HIDDEN = 2816

LANES = 128
SUBLANES = 8
VMEM_LIMIT = 56 * 1024 * 1024

F32 = jnp.float32
BF16 = jnp.bfloat16
HIGHEST = lax.Precision.HIGHEST


def _params(*sem):
    return pltpu.CompilerParams(dimension_semantics=sem, vmem_limit_bytes=VMEM_LIMIT)


def _resident(shape):
    nd = len(shape)
    return pl.BlockSpec(shape, lambda *_: (0,) * nd, pipeline_mode=pl.Buffered(1))


def _dot(a, b):
    return jnp.dot(a, b, preferred_element_type=F32)


def _dot_exact(a, b):
    return jnp.dot(a, b, preferred_element_type=F32, precision=HIGHEST)


def _sigmoid(x):
    return 1.0 / (1.0 + jnp.exp(-x))


def _silu(x):
    return x * _sigmoid(x)


def _softplus(x):
    return jnp.maximum(x, 0.0) + jnp.log(1.0 + jnp.exp(-jnp.abs(x)))


PROJ_SEGS = (
    ("z", 512, F32), ("xbc", 768, F32), ("q", 512, BF16), ("k", 512, BF16), ("v", 512, BF16),
    ("qi", 128, F32), ("small", 128, F32), ("kidx", 128, BF16), ("rwkv", RWKV_COLS, F32), ("s5", 512, F32),
    ("gates", 4096, F32),
)
PROJ_COLS = sum(s[1] for s in PROJ_SEGS)


def _pack_w_in(w):
    o = np.cumsum((0, 512, 768, 8, 512, 512, 512, 128, 32, 4, RWKV_COLS, 512, 4096))
    seg = lambda i: w[:, o[i]:o[i + 1]]
    small = jnp.concatenate([seg(2), seg(7), seg(8), jnp.zeros((D_MODEL, LANES - 44), w.dtype)], axis=1)
    kidx = jnp.tile(seg(7), (1, IDX_HEADS))
    parts = [seg(0), seg(1), seg(3), seg(4), seg(5), seg(6), small, kidx, seg(9), seg(10), seg(11)]
    return jnp.concatenate(parts, axis=1).astype(BF16)


def _proj_kernel(x_ref, g_ref, w_ref, *out_refs):
    x = x_ref[...]
    ms = jnp.mean(x * x, axis=-1, keepdims=True)
    h = (x * lax.rsqrt(ms + RMS_EPS) * g_ref[...]).astype(BF16)
    off = 0
    for o_ref, (_, n, _) in zip(out_refs, PROJ_SEGS):
        o_ref[...] = _dot(h, w_ref[:, off:off + n]).astype(o_ref.dtype)
        off += n


def _proj(x2, g, w_packed, tm=256):
    m = x2.shape[0]
    tm = min(tm, m)
    return pl.pallas_call(
        _proj_kernel,
        grid=(m // tm,),
        in_specs=[pl.BlockSpec((tm, D_MODEL), lambda i: (i, 0)),
                  _resident((1, D_MODEL)), _resident((D_MODEL, PROJ_COLS))],
        out_specs=[pl.BlockSpec((tm, n), lambda i: (i, 0)) for _, n, _ in PROJ_SEGS],
        out_shape=[jax.ShapeDtypeStruct((m, n), dt) for _, n, dt in PROJ_SEGS],
        compiler_params=_params("parallel"),
        name="proj",
    )(x2, g.reshape(1, D_MODEL), w_packed)


def _merge_kernel(x_ref, ya_ref, yb_ref, yc_ref, gc_ref, yd_ref, gate_ref, wb_ref, wo_ref, o_ref):
    ys = (ya_ref[...], yb_ref[...], (yc_ref[...] * gc_ref[...]).astype(BF16), yd_ref[...])
    merged = None
    for n, y in enumerate(ys):
        gate = _sigmoid(gate_ref[:, n * D_MODEL:(n + 1) * D_MODEL])
        term = gate * _dot(y, wb_ref[n])
        merged = term if merged is None else merged + term
    o_ref[...] = x_ref[...] + _dot(merged.astype(BF16), wo_ref[...])


def _merge(x2, y_a, y_b, y_c, g_c, y_d, gates, wb, wo, tm=256):
    m = x2.shape[0]
    tm = min(tm, m)
    row = lambda n: pl.BlockSpec((tm, n), lambda i: (i, 0))
    return pl.pallas_call(
        _merge_kernel,
        grid=(m // tm,),
        in_specs=[row(D_MODEL)] + [row(BRANCH_WIDTH)] * 5 + [row(N_BRANCH * D_MODEL),
                  _resident((N_BRANCH, BRANCH_WIDTH, D_MODEL)), _resident((D_MODEL, D_MODEL))],
        out_specs=row(D_MODEL),
        out_shape=jax.ShapeDtypeStruct((m, D_MODEL), F32),
        compiler_params=_params("parallel"),
        name="merge",
    )(x2, y_a, y_b, y_c, g_c, y_d, gates, wb, wo)


def _ffn_kernel(x_ref, g_ref, w1_ref, w3_ref, w2_ref, gf_ref, o_ref, *, final_norm):
    x = x_ref[...]
    ms = jnp.mean(x * x, axis=-1, keepdims=True)
    h = (x * lax.rsqrt(ms + RMS_EPS) * g_ref[...]).astype(BF16)
    hid = _silu(_dot(h, w1_ref[...])) * _dot(h, w3_ref[...])
    x = x + _dot(hid.astype(BF16), w2_ref[...])
    if final_norm:
        ms = jnp.mean(x * x, axis=-1, keepdims=True)
        x = x * lax.rsqrt(ms + RMS_EPS) * gf_ref[...]
    o_ref[...] = x


def _ffn(x2, g, w1, w3, w2, g_final, final_norm, tm=256):
    m = x2.shape[0]
    tm = min(tm, m)
    row = pl.BlockSpec((tm, D_MODEL), lambda i: (i, 0))
    return pl.pallas_call(
        functools.partial(_ffn_kernel, final_norm=final_norm),
        grid=(m // tm,),
        in_specs=[row, _resident((1, D_MODEL)), _resident((D_MODEL, FFN_HIDDEN)),
                  _resident((D_MODEL, FFN_HIDDEN)), _resident((FFN_HIDDEN, D_MODEL)),
                  _resident((1, D_MODEL))],
        out_specs=row,
        out_shape=jax.ShapeDtypeStruct((m, D_MODEL), F32),
        compiler_params=_params("parallel"),
        name="ffn",
    )(x2, g.reshape(1, D_MODEL), w1, w3, w2, g_final.reshape(1, D_MODEL))


def _head_expand():
    e = np.zeros((LANES, BRANCH_WIDTH), np.float32)
    for h in range(N_HEADS):
        e[h, h * HEAD_DIM:(h + 1) * HEAD_DIM] = 1.0
    return jnp.asarray(e)


def _head_select():
    s = np.zeros((N_HEADS, BRANCH_WIDTH), np.float32)
    for h in range(N_HEADS):
        s[h, h * HEAD_DIM] = 1.0
    return jnp.asarray(s)


def _ssd_kernel(z_ref, xbc_ref, xprev_ref, small_ref, cw_ref, cb_ref, dtb_ref, alog_ref, dsk_ref,
                ng_ref, exp_ref, sel_ref, o_ref, xpad_scr, state_scr, y_scr, *, L):
    c = pl.program_id(1)

    @pl.when(c == 0)
    def _():
        state_scr[...] = jnp.zeros_like(state_scr)

    xpad_scr[0:SUBLANES, :] = jnp.where(c == 0, 0.0, xprev_ref[0])
    xpad_scr[SUBLANES:SUBLANES + L, :] = xbc_ref[0]
    acc = cb_ref[...]
    for j in range(SSD_CONV):
        lo = SUBLANES - (SSD_CONV - 1) + j
        acc = acc + cw_ref[j:j + 1, :] * xpad_scr[lo:lo + L, :]
    xc = _silu(acc)
    xs = xc[:, :BRANCH_WIDTH]

    dt = _softplus(_dot_exact(small_ref[0], exp_ref[...]) + dtb_ref[...])
    a = dt * (-jnp.exp(alog_ref[...]))
    rows = lax.broadcasted_iota(jnp.int32, (L, L), 0)
    cols = lax.broadcasted_iota(jnp.int32, (L, L), 1)
    causal = rows >= cols
    acs = _dot_exact(causal.astype(F32), a)
    acs_t = lax.dot_general(sel_ref[...], acs, (((1,), (1,)), ((), ())),
                            precision=HIGHEST, preferred_element_type=F32)
    xdt = xs * dt

    for g in range(SSD_GROUPS):
        b_g = xc[:, BRANCH_WIDTH + g * SSD_STATE:BRANCH_WIDTH + (g + 1) * SSD_STATE].astype(BF16)
        c_g = xc[:, BRANCH_WIDTH + (SSD_GROUPS + g) * SSD_STATE:
                 BRANCH_WIDTH + (SSD_GROUPS + g + 1) * SSD_STATE].astype(BF16)
        cb = lax.dot_general(c_g, b_g, (((1,), (1,)), ((), ())), preferred_element_type=F32)
        for hh in range(N_HEADS // SSD_GROUPS):
            h = g * (N_HEADS // SSD_GROUPS) + hh
            sl = slice(h * HEAD_DIM, (h + 1) * HEAD_DIM)
            col = acs[:, h * HEAD_DIM:h * HEAD_DIM + 1]
            row = acs_t[h:h + 1, :]
            dec = jnp.where(causal, jnp.exp(jnp.minimum(col - row, 0.0)), 0.0)
            x_h = xdt[:, sl]
            st = state_scr[h]
            y_h = _dot((cb * dec).astype(BF16), x_h.astype(BF16))
            y_h = y_h + _dot(c_g, st.astype(BF16)) * jnp.exp(col)
            y_scr[:, sl] = y_h
            last = acs[L - 1:L, h * HEAD_DIM:h * HEAD_DIM + 1]
            x_w = (x_h * jnp.exp(last - col)).astype(BF16)
            state_scr[h] = st * jnp.exp(last) + lax.dot_general(
                b_g, x_w, (((0,), (0,)), ((), ())), preferred_element_type=F32)

    y = y_scr[...] + dsk_ref[...] * xs
    y = y * _silu(z_ref[0])
    half = BRANCH_WIDTH // SSD_GROUPS
    for g in range(SSD_GROUPS):
        yg = y[:, g * half:(g + 1) * half]
        yg = yg * lax.rsqrt(jnp.mean(yg * yg, axis=-1, keepdims=True) + RMS_EPS)
        o_ref[0, :, g * half:(g + 1) * half] = (yg * ng_ref[:, g * half:(g + 1) * half]).astype(o_ref.dtype)


def _ssd(z, xbc, small, conv_w, conv_b, dt_bias, a_log, d_skip, norm_g, L=64):
    b, t, _ = z.shape
    rep = lambda p: jnp.repeat(p.astype(F32), HEAD_DIM).reshape(1, BRANCH_WIDTH)
    blk = lambda n: pl.BlockSpec((1, L, n), lambda i, j: (i, j, 0))
    prev = pl.BlockSpec((1, SUBLANES, SSD_XBC),
                        lambda i, j: (i, jnp.maximum(j * (L // SUBLANES) - 1, 0), 0))
    return pl.pallas_call(
        functools.partial(_ssd_kernel, L=L),
        grid=(b, t // L),
        in_specs=[blk(BRANCH_WIDTH), blk(SSD_XBC), prev, blk(LANES),
                  _resident((SSD_CONV, SSD_XBC)), _resident((1, SSD_XBC)),
                  _resident((1, BRANCH_WIDTH)), _resident((1, BRANCH_WIDTH)), _resident((1, BRANCH_WIDTH)),
                  _resident((1, BRANCH_WIDTH)), _resident((LANES, BRANCH_WIDTH)),
                  _resident((N_HEADS, BRANCH_WIDTH))],
        out_specs=blk(BRANCH_WIDTH),
        out_shape=jax.ShapeDtypeStruct((b, t, BRANCH_WIDTH), BF16),
        scratch_shapes=[pltpu.VMEM((SUBLANES + L, SSD_XBC), F32),
                        pltpu.VMEM((N_HEADS, SSD_STATE, HEAD_DIM), F32),
                        pltpu.VMEM((L, BRANCH_WIDTH), F32)],
        compiler_params=_params("parallel", "arbitrary"),
        name="ssd",
    )(z, xbc, xbc, small, conv_w, conv_b.reshape(1, SSD_XBC), rep(dt_bias), rep(a_log), rep(d_skip),
      norm_g.reshape(1, BRANCH_WIDTH), _head_expand(), _head_select())


S5_CHUNK = 64
S5_ROW = S5_CHUNK * S5_GROUP


def _s5_tables(a_re, a_im, b_re, b_im, c_re, c_im, log_dt):
    L = S5_CHUNK
    f = lambda v: v.astype(F32)
    a_re, a_im, b_re, b_im, c_re, c_im = map(f, (a_re, a_im, b_re, b_im, c_re, c_im))
    dt = jnp.exp(f(log_dt))[:, None]
    lr, li = a_re * dt, a_im * dt
    tau = jnp.arange(L + 1, dtype=F32)[None, :, None]
    mag = jnp.exp(lr[:, None, :] * tau)
    pr, pi = mag * jnp.cos(li[:, None, :] * tau), mag * jnp.sin(li[:, None, :] * tau)
    abr, abi = pr[:, 1], pi[:, 1]
    den = a_re * a_re + a_im * a_im
    qr = ((abr - 1.0) * a_re + abi * a_im) / den
    qi = (abi * a_re - (abr - 1.0) * a_im) / den
    bbr = qr[..., None] * b_re - qi[..., None] * b_im
    bbi = qr[..., None] * b_im + qi[..., None] * b_re
    pbr = pr[..., None] * bbr[:, None] - pi[..., None] * bbi[:, None]
    pbi = pr[..., None] * bbi[:, None] + pi[..., None] * bbr[:, None]
    kern = (jnp.einsum("gop,gtpi->gtoi", c_re, pbr[:, :L], precision=HIGHEST)
            - jnp.einsum("gop,gtpi->gtoi", c_im, pbi[:, :L], precision=HIGHEST))
    lag = jnp.arange(L)[None, :] - jnp.arange(L)[:, None]
    tz = jnp.where((lag >= 0)[None, :, :, None, None], kern[:, jnp.maximum(lag, 0)], 0.0)
    tz = tz.transpose(0, 1, 4, 2, 3).reshape(S5_GROUPS, S5_ROW, S5_ROW)
    rev = L - 1 - jnp.arange(L)
    ws = jnp.concatenate([pbr[:, rev], pbi[:, rev]], axis=2)
    ws = ws.transpose(0, 1, 3, 2).reshape(S5_GROUPS, S5_ROW, 2 * S5_STATE)
    cpr = c_re[:, None] * pr[:, 1:, None, :] - c_im[:, None] * pi[:, 1:, None, :]
    cpi = c_re[:, None] * pi[:, 1:, None, :] + c_im[:, None] * pr[:, 1:, None, :]
    wy = jnp.concatenate([cpr, -cpi], axis=3).transpose(0, 3, 1, 2).reshape(S5_GROUPS, 2 * S5_STATE, S5_ROW)
    a1 = jnp.concatenate([pr[:, L], pr[:, L]], axis=1)[:, None, :]
    a2 = jnp.concatenate([-pi[:, L], pi[:, L]], axis=1)[:, None, :]
    return tz.astype(BF16), ws.astype(BF16), wy.astype(BF16), a1, a2


def _s5_kernel(u_ref, tz_ref, ws_ref, wy_ref, a1_ref, a2_ref, y_ref, sl_scr, x_scr, *, nb, nc):
    u = u_ref[0]
    y_local = _dot(u, tz_ref[0])
    sl_scr[...] = _dot(u, ws_ref[0])
    a1, a2 = a1_ref[0], a2_ref[0]

    def step(c, x):
        r = pl.ds(pl.multiple_of(c * nb, SUBLANES), nb)
        x_scr[r, :] = x
        return a1 * x + a2 * pltpu.roll(x, S5_STATE, axis=1) + sl_scr[r, :]

    lax.fori_loop(0, nc, step, jnp.zeros((nb, 2 * S5_STATE), F32))
    y_ref[0] = y_local + _dot(x_scr[...].astype(BF16), wy_ref[0])


def _s5_out_kernel(y_ref, u_ref, d_ref, w_ref, b_ref, o_ref):
    y = y_ref[...] + d_ref[...] * u_ref[...]
    y = 0.5 * y * (1.0 + jnp.tanh(math.sqrt(2.0 / math.pi) * (y + 0.044715 * (y * y * y))))
    o_ref[...] = (y * _sigmoid(_dot(y.astype(BF16), w_ref[...]) + b_ref[...])).astype(o_ref.dtype)


def _s5(u, tables, d_skip, glu_w, glu_b, tm=512):
    b, t, _ = u.shape
    nc = t // S5_CHUNK
    m = nc * b
    tz, ws, wy, a1, a2 = tables
    ug = u.reshape(b, nc, S5_CHUNK, S5_GROUPS, S5_GROUP).transpose(3, 1, 0, 2, 4)
    ug = ug.reshape(S5_GROUPS, m, S5_ROW).astype(BF16)
    grp = lambda r, c: pl.BlockSpec((1, r, c), lambda g: (g, 0, 0))
    y = pl.pallas_call(
        functools.partial(_s5_kernel, nb=b, nc=nc),
        grid=(S5_GROUPS,),
        in_specs=[grp(m, S5_ROW), grp(S5_ROW, S5_ROW), grp(S5_ROW, 2 * S5_STATE),
                  grp(2 * S5_STATE, S5_ROW), grp(1, 2 * S5_STATE), grp(1, 2 * S5_STATE)],
        out_specs=grp(m, S5_ROW),
        out_shape=jax.ShapeDtypeStruct((S5_GROUPS, m, S5_ROW), F32),
        scratch_shapes=[pltpu.VMEM((m, 2 * S5_STATE), F32), pltpu.VMEM((m, 2 * S5_STATE), F32)],
        compiler_params=_params("parallel"),
        name="s5_scan",
    )(ug, tz, ws, wy, a1, a2)
    y = y.reshape(S5_GROUPS, nc, b, S5_CHUNK, S5_GROUP).transpose(2, 1, 3, 0, 4).reshape(b * t, BRANCH_WIDTH)
    tm = min(tm, b * t)
    row = pl.BlockSpec((tm, BRANCH_WIDTH), lambda i: (i, 0))
    return pl.pallas_call(
        _s5_out_kernel,
        grid=(b * t // tm,),
        in_specs=[row, row, _resident((1, BRANCH_WIDTH)), _resident((BRANCH_WIDTH, BRANCH_WIDTH)),
                  _resident((1, BRANCH_WIDTH))],
        out_specs=row,
        out_shape=jax.ShapeDtypeStruct((b * t, BRANCH_WIDTH), BF16),
        compiler_params=_params("parallel"),
        name="s5_out",
    )(y, u.reshape(b * t, BRANCH_WIDTH), d_skip.reshape(1, BRANCH_WIDTH).astype(F32), glu_w.astype(BF16),
      glu_b.reshape(1, BRANCH_WIDTH).astype(F32))


RWKV_LORA_OFF = 3 * BRANCH_WIDTH
RWKV_GATE_OFF = RWKV_LORA_OFF + LANES


def _rwkv_pre_kernel(p_ref, prev_ref, mu_ref, w0_ref, w2_ref, a0_ref, a2_ref, g2_ref, kk_ref, ka_ref,
                     r_o, w_o, k_o, v_o, kk_o, a_o, g_o, pad_scr, *, tm):
    j = pl.program_id(1)
    pad_scr[0:SUBLANES, :] = jnp.where(j == 0, 0.0, prev_ref[0])
    pad_scr[SUBLANES:SUBLANES + tm, :] = p_ref[0]
    p = p_ref[0]
    p = p + (pad_scr[SUBLANES - 1:SUBLANES - 1 + tm, :] - p) * mu_ref[...]
    r = p[:, 0:BRANCH_WIDTH]
    k = p[:, BRANCH_WIDTH:2 * BRANCH_WIDTH]
    v = p[:, 2 * BRANCH_WIDTH:3 * BRANCH_WIDTH]
    lora = p[:, RWKV_LORA_OFF:RWKV_GATE_OFF]
    dg = p[:, RWKV_GATE_OFF:RWKV_COLS]
    w = -_softplus(-(w0_ref[...] + _dot(jnp.tanh(lora).astype(BF16), w2_ref[...]))) - 0.5
    a = _sigmoid(a0_ref[...] + _dot(lora.astype(BF16), a2_ref[...]))
    r_o[0] = r
    w_o[0] = jnp.exp(-jnp.exp(w))
    k_o[0] = k * (1.0 + (a - 1.0) * ka_ref[...])
    v_o[0] = v
    kk_o[0] = k * kk_ref[...]
    a_o[0] = a
    g_o[0] = _dot(_sigmoid(dg).astype(BF16), g2_ref[...])


def _rwkv_scan_kernel(r_ref, w_ref, k_ref, v_ref, kk_ref, a_ref, rk_ref, lng_ref, lnb_ref, y_ref,
                      s_scr, r_scr, w_scr, k_scr, v_scr, av_scr, bv_scr, y_scr, *, tb):
    nl = s_scr.shape[2]

    @pl.when(pl.program_id(0) == 0)
    def _():
        s_scr[...] = jnp.zeros_like(s_scr)

    def relayout(t, carry):
        tile = lambda ref: ref[:, t].reshape(nl, HEAD_DIM).T
        r_scr[t] = tile(r_ref)
        w_scr[t] = tile(w_ref)
        k_scr[t] = tile(k_ref)
        v_scr[t] = tile(v_ref)
        kk = tile(kk_ref)
        kk = kk / jnp.maximum(jnp.sqrt(jnp.sum(kk * kk, axis=0, keepdims=True)), 1e-12)
        av_scr[t] = -kk
        bv_scr[t] = kk * tile(a_ref)
        return carry

    lax.fori_loop(0, tb, relayout, 0)

    def step(t, carry):
        sa = jnp.zeros(s_scr.shape[1:], F32)
        for kx in range(HEAD_DIM):
            sa = sa + s_scr[kx] * av_scr[t, kx:kx + 1, :]
        v_t = v_scr[t]
        y = jnp.zeros(s_scr.shape[1:], F32)
        for kx in range(HEAD_DIM):
            s_new = (s_scr[kx] * w_scr[t, kx:kx + 1, :] + sa * bv_scr[t, kx:kx + 1, :]
                     + v_t * k_scr[t, kx:kx + 1, :])
            s_scr[kx] = s_new
            y = y + s_new * r_scr[t, kx:kx + 1, :]
        y_scr[t] = y
        return carry

    lax.fori_loop(0, tb, step, 0)

    y = y_scr[...]
    mean = jnp.mean(y, axis=1, keepdims=True)
    var = jnp.mean(jnp.square(y - mean), axis=1, keepdims=True)
    y = (y - mean) * lax.rsqrt(var + RWKV_GN_EPS) * lng_ref[...] + lnb_ref[...]
    bonus = jnp.sum(r_scr[...] * k_scr[...] * rk_ref[...], axis=1, keepdims=True) * v_scr[...]
    y_scr[...] = y + bonus

    def writeback(t, carry):
        y_ref[:, t] = y_scr[t].T.reshape(nl // N_HEADS, N_HEADS, HEAD_DIM)
        return carry

    lax.fori_loop(0, tb, writeback, 0)


def _rwkv(cols, mu, w0, w2, a0, a2, g2, k_k, k_a, r_k, ln_g, ln_b, tm=256, tb=32):
    b, t, _ = cols.shape
    tm, tb = min(tm, t), min(tb, t)
    vec = lambda p: p.reshape(1, -1).astype(F32)
    zpad = jnp.zeros((LANES // 2, BRANCH_WIDTH), F32)
    blk = lambda n: pl.BlockSpec((1, tm, n), lambda i, j: (i, j, 0))
    prev = pl.BlockSpec((1, SUBLANES, RWKV_COLS),
                        lambda i, j: (i, jnp.maximum(j * (tm // SUBLANES) - 1, 0), 0))
    outs = pl.pallas_call(
        functools.partial(_rwkv_pre_kernel, tm=tm),
        grid=(b, t // tm),
        in_specs=[blk(RWKV_COLS), prev, _resident((1, RWKV_COLS)), _resident((1, BRANCH_WIDTH)),
                  _resident((LANES, BRANCH_WIDTH)), _resident((1, BRANCH_WIDTH)),
                  _resident((LANES, BRANCH_WIDTH)), _resident((LANES, BRANCH_WIDTH)),
                  _resident((1, BRANCH_WIDTH)), _resident((1, BRANCH_WIDTH))],
        out_specs=[blk(BRANCH_WIDTH)] * 7,
        out_shape=[jax.ShapeDtypeStruct((b, t, BRANCH_WIDTH), F32)] * 7,
        scratch_shapes=[pltpu.VMEM((SUBLANES + tm, RWKV_COLS), F32)],
        compiler_params=_params("parallel", "arbitrary"),
        name="rwkv_pre",
    )(cols, cols, vec(mu), vec(w0), jnp.concatenate([w2.astype(F32), zpad]).astype(BF16), vec(a0),
      jnp.concatenate([zpad, a2.astype(F32)]).astype(BF16), g2.astype(BF16), vec(k_k), vec(k_a))
    *seqs, gate = outs
    per_head = lambda p: jnp.tile(p.astype(F32).reshape(N_HEADS, HEAD_DIM).T, (1, b))[None]
    nl = b * N_HEADS
    heads = lambda s: s.reshape(b, t, N_HEADS, HEAD_DIM)
    tblk = pl.BlockSpec((b, tb, N_HEADS, HEAD_DIM), lambda i: (0, i, 0, 0))
    tile = pltpu.VMEM((tb, HEAD_DIM, nl), F32)
    y = pl.pallas_call(
        functools.partial(_rwkv_scan_kernel, tb=tb),
        grid=(t // tb,),
        in_specs=[tblk] * 6 + [_resident((1, HEAD_DIM, nl))] * 3,
        out_specs=tblk,
        out_shape=jax.ShapeDtypeStruct((b, t, N_HEADS, HEAD_DIM), F32),
        scratch_shapes=[pltpu.VMEM((HEAD_DIM, HEAD_DIM, nl), F32)] + [tile] * 7,
        compiler_params=_params("arbitrary"),
        name="rwkv_scan",
    )(*[heads(s) for s in seqs], per_head(r_k), per_head(ln_g), per_head(ln_b))
    return y.reshape(b * t, BRANCH_WIDTH), gate.reshape(b * t, BRANCH_WIDTH)


DSA_TQ = 128
DSA_TK = 512
INT_MIN = -2 ** 31
MASKED = -1e30


def _bias_tiles():
    n = np.arange(1, 8192, dtype=np.float32)
    large = 8 + (np.log(n / np.float32(8)) / np.float32(math.log(REL_MAX_DIST / 8)) * np.float32(8)).astype(np.int32)
    n_sat = int(n[np.argmax(large >= 15)])
    return -(-(n_sat + DSA_TQ - 1) // DSA_TQ) + 1


def _bias_kernel(rb_ref, o_ref):
    e = pl.program_id(0)
    t = DSA_TQ
    rel = (lax.broadcasted_iota(jnp.int32, (t, t), 0) - lax.broadcasted_iota(jnp.int32, (t, t), 1)) - e * t
    half, exact = REL_BUCKETS // 2, REL_BUCKETS // 4
    n = jnp.abs(rel)
    n_f = jnp.maximum(n, 1).astype(F32)
    large = exact + (jnp.log(n_f / exact) / math.log(REL_MAX_DIST / exact) * (half - exact)).astype(jnp.int32)
    large = jnp.minimum(large, half - 1)
    bucket = jnp.where(rel > 0, half, 0) + jnp.where(n < exact, n, large)
    for h in range(N_HEADS):
        acc = jnp.zeros((t, t), F32)
        for bkt in range(REL_BUCKETS):
            acc = jnp.where(bucket == bkt, rb_ref[bkt, h], acc)
        o_ref[h // 2, 0, :, (h % 2) * t:(h % 2 + 1) * t] = acc


def _rel_bias_table(rel_bias):
    nd = _bias_tiles()
    t = DSA_TQ
    return pl.pallas_call(
        _bias_kernel,
        grid=(nd,),
        in_specs=[pl.BlockSpec(memory_space=pltpu.SMEM)],
        out_specs=pl.BlockSpec((N_HEADS // 2, 1, t, 2 * t), lambda e: (0, e, 0, 0)),
        out_shape=jax.ShapeDtypeStruct((N_HEADS // 2, nd, t, 2 * t), F32),
        compiler_params=_params("parallel"),
        name="rel_bias",
    )(rel_bias.astype(F32))


def _nt_dot(a, b, **kw):
    return lax.dot_general(a, b, (((1,), (1,)), ((), ())), preferred_element_type=F32, **kw)


def _dsa_kernel(q_ref, k_ref, vt_ref, qi_ref, kidx_ref, small_ref, bias_ref, tri_ref, pick_ref, o_ref,
                key_scr, msk_scr, lga_scr, lgb_scr, *, topk, nd):
    tq, tk = DSA_TQ, DSA_TK
    th = tk // 2
    q0 = pl.program_id(1) * tq
    nkb = (q0 + tq + tk - 1) // tk
    int_min = jnp.int32(INT_MIN)
    q_pos = q0 + lax.broadcasted_iota(jnp.int32, (1, tq), 1)
    lim = (q_pos // CHUNK + 1) * CHUNK
    lane = lax.broadcasted_iota(jnp.int32, (tq, LANES), 1)

    qi = qi_ref[0]
    qi_all = jnp.concatenate(
        [jnp.where((lane >= h * IDX_DIM) & (lane < (h + 1) * IDX_DIM), qi, 0.0).astype(BF16)
         for h in range(IDX_HEADS)], axis=0)
    w_t = _nt_dot(pick_ref[...], small_ref[0], precision=HIGHEST)
    w_h = [w_t[h:h + 1, :] for h in range(IDX_HEADS)]

    def score_block(j, carry):
        s0 = pl.multiple_of(j * tk, tk)
        kidx = kidx_ref[0, pl.ds(s0, tk), :]
        dots = _nt_dot(kidx, qi_all)
        s = None
        for h in range(IDX_HEADS):
            term = w_h[h] * jnp.maximum(dots[:, h * tq:(h + 1) * tq], 0.0)
            s = term if s is None else s + term
        s = s + 0.0
        bits = lax.bitcast_convert_type(s, jnp.int32)
        bits = jnp.where(bits < 0, bits ^ jnp.int32(0x7FFFFFFF), bits)
        pos = s0 + lax.broadcasted_iota(jnp.int32, (tk, tq), 0)
        key_scr[pl.ds(s0, tk), :] = jnp.where(pos < lim, bits, int_min)
        return carry

    lax.fori_loop(0, nkb, score_block, 0)

    acc_rows = 4 * SUBLANES

    def count(pred):
        def body(j, acc):
            hit = jnp.where(pred(key_scr[pl.ds(pl.multiple_of(j * tk, tk), tk), :]), 1.0, 0.0)
            return acc + jnp.sum(hit.reshape(tk // acc_rows, acc_rows, tq), axis=0)
        acc = lax.fori_loop(0, nkb, body, jnp.zeros((acc_rows, tq), F32))
        return jnp.sum(acc, axis=0, keepdims=True)

    kf = jnp.float32(topk)
    c0 = count(lambda kb: kb >= 0)
    thr = jnp.where(c0 >= kf, jnp.int32(0), int_min)
    n_ge = jnp.where(c0 >= kf, c0, 0.0)

    def search(it, carry):
        thr, n_ge = carry
        cand = thr + jnp.left_shift(jnp.int32(1), 30 - it)
        c = count(lambda kb: kb >= cand)
        return jnp.where(c >= kf, cand, thr), jnp.where(c >= kf, c, n_ge)

    thr, n_ge = lax.fori_loop(0, 31, search, (thr, n_ge))
    real = thr > int_min
    surplus = jnp.max(jnp.where(real, n_ge - kf, 0.0))

    def select_plain(j, carry):
        s0 = pl.multiple_of(j * tk, tk)
        kb = key_scr[pl.ds(s0, tk), :]
        msk_scr[pl.ds(s0, tk), :] = jnp.where((kb >= thr) & (kb > int_min), 0.0, MASKED)
        return carry

    @pl.when(surplus <= 0.0)
    def _():
        lax.fori_loop(0, nkb, select_plain, 0)

    @pl.when(surplus > 0.0)
    def _():
        quota = kf - count(lambda kb: kb > thr)

        def select_ranked(j, run):
            s0 = pl.multiple_of(j * tk, tk)
            kb = key_scr[pl.ds(s0, tk), :]
            eq = kb == thr
            rank = run + _dot(tri_ref[...], jnp.where(eq, 1.0, 0.0).astype(BF16))
            sel = (kb > thr) | (eq & real & (rank <= quota))
            msk_scr[pl.ds(s0, tk), :] = jnp.where(sel, 0.0, MASKED)
            return rank[tk - 1:tk, :]

        lax.fori_loop(0, nkb, select_ranked, jnp.zeros((1, tq), F32))

    q = q_ref[0]
    dim = lax.broadcasted_iota(jnp.int32, (LANES, tq), 0)
    for hp in range(N_HEADS // 2):
        q_pair = q[:, hp * LANES:(hp + 1) * LANES] * HEAD_DIM ** -0.5
        q2 = jnp.concatenate([jnp.where(lane < HEAD_DIM, q_pair, 0.0).astype(BF16),
                              jnp.where(lane >= HEAD_DIM, q_pair, 0.0).astype(BF16)], axis=0)

        def raw_logits(jh, hp=hp, q2=q2):
            s0 = pl.multiple_of(jh * th, th)
            return _nt_dot(k_ref[0, pl.ds(s0, th), hp * LANES:(hp + 1) * LANES], q2)

        def finish_logits(lg, jh, buf, hp=hp):
            s0 = pl.multiple_of(jh * th, th)
            e0 = (q0 - s0) // tq
            top = None
            for c in range(th // LANES):
                mask = msk_scr[pl.ds(s0 + c * LANES, LANES), :]
                lg_c = (lg[c * LANES:(c + 1) * LANES] + bias_ref[hp, jnp.clip(e0 - c, 0, nd - 1)]
                        + jnp.concatenate([mask, mask], axis=1))
                buf[c * LANES:(c + 1) * LANES, :] = lg_c
                top_c = jnp.max(lg_c, axis=0, keepdims=True)
                top = top_c if top is None else jnp.maximum(top, top_c)
            return top

        def softmax_pv(state, buf, top, j, half, hp=hp):
            m, l, acc = state
            m_new = jnp.maximum(m, top)
            alpha = jnp.exp(m - m_new)
            p = jnp.exp(buf[...] - m_new)
            l = alpha * l + jnp.sum(p, axis=0, keepdims=True)
            vt_blk = vt_ref[0, j, hp * LANES:(hp + 1) * LANES, half * th:(half + 1) * th]
            return m_new, l, alpha * acc + _dot(vt_blk, p.astype(BF16))

        def attend(j, carry):
            *state, top_a, top_b = carry
            ja = jnp.minimum(2 * j + 2, 2 * nkb - 2)
            lg_a, lg_b = raw_logits(ja), raw_logits(ja + 1)
            state = softmax_pv(state, lga_scr, top_a, j, 0)
            state = softmax_pv(state, lgb_scr, top_b, j, 1)
            return (*state, finish_logits(lg_a, ja, lga_scr), finish_logits(lg_b, ja + 1, lgb_scr))

        init = (jnp.full((1, 2 * tq), MASKED, F32), jnp.zeros((1, 2 * tq), F32), jnp.zeros((LANES, 2 * tq), F32),
                finish_logits(raw_logits(0), 0, lga_scr), finish_logits(raw_logits(1), 1, lgb_scr))
        _, l, acc, _, _ = lax.fori_loop(0, nkb, attend, init)
        o2 = acc / l
        out_t = jnp.where(dim < HEAD_DIM, o2[:, :tq], o2[:, tq:])
        o_ref[0, :, hp * LANES:(hp + 1) * LANES] = out_t.T.astype(o_ref.dtype)


def _dsa(q, k, v, qi, kidx, small, bias_tab):
    b, t, _ = q.shape
    topk = min(DSA_MAX_TOPK, t // 4)
    nd = bias_tab.shape[1]
    nblk = t // DSA_TK
    vt = v.reshape(b, nblk, DSA_TK, BRANCH_WIDTH).transpose(0, 1, 3, 2)
    tri = jnp.asarray(np.tril(np.ones((DSA_TK, DSA_TK), np.float32))).astype(BF16)
    pick = np.zeros((SUBLANES, LANES), np.float32)
    pick[np.arange(IDX_HEADS), 40 + np.arange(IDX_HEADS)] = 1.0
    qblk = lambda n: pl.BlockSpec((1, DSA_TQ, n), lambda i, j: (i, j, 0))
    full = lambda *s: pl.BlockSpec((1,) + s, lambda i, j: (i,) + (0,) * len(s))
    return pl.pallas_call(
        functools.partial(_dsa_kernel, topk=topk, nd=nd),
        grid=(b, t // DSA_TQ),
        in_specs=[qblk(BRANCH_WIDTH), full(t, BRANCH_WIDTH), full(nblk, BRANCH_WIDTH, DSA_TK), qblk(LANES),
                  full(t, LANES), qblk(LANES), _resident(bias_tab.shape), _resident((DSA_TK, DSA_TK)),
                  _resident((SUBLANES, LANES))],
        out_specs=qblk(BRANCH_WIDTH),
        out_shape=jax.ShapeDtypeStruct((b, t, BRANCH_WIDTH), BF16),
        scratch_shapes=[pltpu.VMEM((t, DSA_TQ), jnp.int32), pltpu.VMEM((t, DSA_TQ), F32),
                        pltpu.VMEM((DSA_TK // 2, 2 * DSA_TQ), F32), pltpu.VMEM((DSA_TK // 2, 2 * DSA_TQ), F32)],
        compiler_params=_params("parallel", "arbitrary"),
        name="dsa",
    )(q, k, vt, qi, kidx, small, bias_tab, tri, jnp.asarray(pick))


def kernel(x, rel_bias, norm_mix_g, w_in, ssd_conv_w, ssd_conv_b, ssd_dt_bias, ssd_a_log, ssd_d, ssd_norm_g,
           rwkv_mu, rwkv_w0, rwkv_w2, rwkv_a0, rwkv_a2, rwkv_g2, rwkv_k_k, rwkv_k_a, rwkv_r_k, rwkv_ln_g,
           rwkv_ln_b, s5_a_re, s5_a_im, s5_b_re, s5_b_im, s5_c_re, s5_c_im, s5_d, s5_log_dt, s5_glu_w,
           s5_glu_b, w_branch, w_out, norm_ffn_g, ffn_w1, ffn_w3, ffn_w2, norm_final_g):
    b, t, _ = x.shape
    x2 = x.reshape(b * t, D_MODEL).astype(F32)
    seq = lambda a: a.reshape(b, t, a.shape[-1])
    bias_tab = _rel_bias_table(rel_bias)
    for i in range(DEPTH):
        z, xbc, q, k, v, qi, small, kidx, rwkv_cols, s5_u, gates = _proj(x2, norm_mix_g[i], _pack_w_in(w_in[i]))
        small = seq(small)
        y_a = _ssd(seq(z), seq(xbc), small, ssd_conv_w[i], ssd_conv_b[i], ssd_dt_bias[i], ssd_a_log[i],
                   ssd_d[i], ssd_norm_g[i])
        y_b = _dsa(seq(q), seq(k), seq(v), seq(qi), seq(kidx), small, bias_tab)
        y_c, g_c = _rwkv(seq(rwkv_cols), rwkv_mu[i], rwkv_w0[i], rwkv_w2[i], rwkv_a0[i], rwkv_a2[i],
                         rwkv_g2[i], rwkv_k_k[i], rwkv_k_a[i], rwkv_r_k[i], rwkv_ln_g[i], rwkv_ln_b[i])
        tables = _s5_tables(s5_a_re[i], s5_a_im[i], s5_b_re[i], s5_b_im[i], s5_c_re[i], s5_c_im[i],
                            s5_log_dt[i])
        y_d = _s5(seq(s5_u), tables, s5_d[i], s5_glu_w[i], s5_glu_b[i])
        x2 = _merge(x2, y_a.reshape(b * t, BRANCH_WIDTH), y_b.reshape(b * t, BRANCH_WIDTH), y_c, g_c, y_d,
                    gates, w_branch[i].astype(BF16), w_out[i].astype(BF16))
        x2 = _ffn(x2, norm_ffn_g[i], ffn_w1[i].astype(BF16), ffn_w3[i].astype(BF16), ffn_w2[i].astype(BF16),
                  norm_final_g, final_norm=(i == DEPTH - 1))
    return x2.reshape(b, t, D_MODEL)
```

```python
import functools
import math

import jax
import jax.numpy as jnp
import numpy as np
from jax import lax
from jax.experimental import pallas as pl
from jax.experimental.pallas import tpu as pltpu

D_MODEL = 1024
DEPTH = 4
CHUNK = 64
N_BRANCH = 4
BRANCH_WIDTH = 512
HEAD_DIM = 64
N_HEADS = 8
RMS_EPS = 1e-6

SSD_GROUPS = 2
SSD_STATE = 64
SSD_CONV = 4
SSD_XBC = 768

IDX_HEADS = 4
IDX_DIM = 32
DSA_MAX_TOPK = 256
REL_BUCKETS = 32
REL_MAX_DIST = 256

RWKV_COLS = 1792
RWKV_GN_EPS = 64e-5

S5_GROUP = 16
S5_GROUPS = 32
S5_STATE = 64

FFN_HIDDEN = 2816

LANES = 128
SUBLANES = 8
VMEM_LIMIT = 56 * 1024 * 1024

F32 = jnp.float32
BF16 = jnp.bfloat16
HIGHEST = lax.Precision.HIGHEST


def _params(*sem):
    return pltpu.CompilerParams(dimension_semantics=sem, vmem_limit_bytes=VMEM_LIMIT)


def _resident(shape):
    nd = len(shape)
    return pl.BlockSpec(shape, lambda *_: (0,) * nd, pipeline_mode=pl.Buffered(1))


def _dot(a, b):
    return jnp.dot(a, b, preferred_element_type=F32)


def _dot_exact(a, b):
    return jnp.dot(a, b, preferred_element_type=F32, precision=HIGHEST)


def _sigmoid(x):
    return 1.0 / (1.0 + jnp.exp(-x))


def _silu(x):
    return x * _sigmoid(x)


def _softplus(x):
    return jnp.maximum(x, 0.0) + jnp.log(1.0 + jnp.exp(-jnp.abs(x)))


PROJ_SEGS = (
    ("z", 512, F32), ("xbc", 768, F32), ("q", 512, BF16), ("k", 512, BF16), ("v", 512, BF16),
    ("qi", 128, F32), ("small", 128, F32), ("kidx", 128, BF16), ("rwkv", RWKV_COLS, F32), ("s5", 512, F32),
    ("gates", 4096, F32),
)
PROJ_COLS = sum(s[1] for s in PROJ_SEGS)


def _pack_w_in(w):
    o = np.cumsum((0, 512, 768, 8, 512, 512, 512, 128, 32, 4, RWKV_COLS, 512, 4096))
    seg = lambda i: w[:, o[i]:o[i + 1]]
    small = jnp.concatenate([seg(2), seg(7), seg(8), jnp.zeros((D_MODEL, LANES - 44), w.dtype)], axis=1)
    kidx = jnp.tile(seg(7), (1, IDX_HEADS))
    parts = [seg(0), seg(1), seg(3), seg(4), seg(5), seg(6), small, kidx, seg(9), seg(10), seg(11)]
    return jnp.concatenate(parts, axis=1).astype(BF16)


def _proj_kernel(x_ref, g_ref, w_ref, *out_refs):
    x = x_ref[...]
    ms = jnp.mean(x * x, axis=-1, keepdims=True)
    h = (x * lax.rsqrt(ms + RMS_EPS) * g_ref[...]).astype(BF16)
    off = 0
    for o_ref, (_, n, _) in zip(out_refs, PROJ_SEGS):
        o_ref[...] = _dot(h, w_ref[:, off:off + n]).astype(o_ref.dtype)
        off += n


def _proj(x2, g, w_packed, tm=256):
    m = x2.shape[0]
    tm = min(tm, m)
    return pl.pallas_call(
        _proj_kernel,
        grid=(m // tm,),
        in_specs=[pl.BlockSpec((tm, D_MODEL), lambda i: (i, 0)),
                  _resident((1, D_MODEL)), _resident((D_MODEL, PROJ_COLS))],
        out_specs=[pl.BlockSpec((tm, n), lambda i: (i, 0)) for _, n, _ in PROJ_SEGS],
        out_shape=[jax.ShapeDtypeStruct((m, n), dt) for _, n, dt in PROJ_SEGS],
        compiler_params=_params("parallel"),
        name="proj",
    )(x2, g.reshape(1, D_MODEL), w_packed)


def _merge_kernel(x_ref, ya_ref, yb_ref, yc_ref, gc_ref, yd_ref, gate_ref, wb_ref, wo_ref, o_ref):
    ys = (ya_ref[...], yb_ref[...], (yc_ref[...] * gc_ref[...]).astype(BF16), yd_ref[...])
    merged = None
    for n, y in enumerate(ys):
        gate = _sigmoid(gate_ref[:, n * D_MODEL:(n + 1) * D_MODEL])
        term = gate * _dot(y, wb_ref[n])
        merged = term if merged is None else merged + term
    o_ref[...] = x_ref[...] + _dot(merged.astype(BF16), wo_ref[...])


def _merge(x2, y_a, y_b, y_c, g_c, y_d, gates, wb, wo, tm=256):
    m = x2.shape[0]
    tm = min(tm, m)
    row = lambda n: pl.BlockSpec((tm, n), lambda i: (i, 0))
    return pl.pallas_call(
        _merge_kernel,
        grid=(m // tm,),
        in_specs=[row(D_MODEL)] + [row(BRANCH_WIDTH)] * 5 + [row(N_BRANCH * D_MODEL),
                  _resident((N_BRANCH, BRANCH_WIDTH, D_MODEL)), _resident((D_MODEL, D_MODEL))],
        out_specs=row(D_MODEL),
        out_shape=jax.ShapeDtypeStruct((m, D_MODEL), F32),
        compiler_params=_params("parallel"),
        name="merge",
    )(x2, y_a, y_b, y_c, g_c, y_d, gates, wb, wo)


def _ffn_kernel(x_ref, g_ref, w1_ref, w3_ref, w2_ref, gf_ref, o_ref, *, final_norm):
    x = x_ref[...]
    ms = jnp.mean(x * x, axis=-1, keepdims=True)
    h = (x * lax.rsqrt(ms + RMS_EPS) * g_ref[...]).astype(BF16)
    hid = _silu(_dot(h, w1_ref[...])) * _dot(h, w3_ref[...])
    x = x + _dot(hid.astype(BF16), w2_ref[...])
    if final_norm:
        ms = jnp.mean(x * x, axis=-1, keepdims=True)
        x = x * lax.rsqrt(ms + RMS_EPS) * gf_ref[...]
    o_ref[...] = x


def _ffn(x2, g, w1, w3, w2, g_final, final_norm, tm=256):
    m = x2.shape[0]
    tm = min(tm, m)
    row = pl.BlockSpec((tm, D_MODEL), lambda i: (i, 0))
    return pl.pallas_call(
        functools.partial(_ffn_kernel, final_norm=final_norm),
        grid=(m // tm,),
        in_specs=[row, _resident((1, D_MODEL)), _resident((D_MODEL, FFN_HIDDEN)),
                  _resident((D_MODEL, FFN_HIDDEN)), _resident((FFN_HIDDEN, D_MODEL)),
                  _resident((1, D_MODEL))],
        out_specs=row,
        out_shape=jax.ShapeDtypeStruct((m, D_MODEL), F32),
        compiler_params=_params("parallel"),
        name="ffn",
    )(x2, g.reshape(1, D_MODEL), w1, w3, w2, g_final.reshape(1, D_MODEL))


def _head_expand():
    e = np.zeros((LANES, BRANCH_WIDTH), np.float32)
    for h in range(N_HEADS):
        e[h, h * HEAD_DIM:(h + 1) * HEAD_DIM] = 1.0
    return jnp.asarray(e)


def _head_select():
    s = np.zeros((N_HEADS, BRANCH_WIDTH), np.float32)
    for h in range(N_HEADS):
        s[h, h * HEAD_DIM] = 1.0
    return jnp.asarray(s)


def _ssd_kernel(z_ref, xbc_ref, xprev_ref, small_ref, cw_ref, cb_ref, dtb_ref, alog_ref, dsk_ref,
                ng_ref, exp_ref, sel_ref, o_ref, xpad_scr, state_scr, y_scr, *, L):
    c = pl.program_id(1)

    @pl.when(c == 0)
    def _():
        state_scr[...] = jnp.zeros_like(state_scr)

    xpad_scr[0:SUBLANES, :] = jnp.where(c == 0, 0.0, xprev_ref[0])
    xpad_scr[SUBLANES:SUBLANES + L, :] = xbc_ref[0]
    acc = cb_ref[...]
    for j in range(SSD_CONV):
        lo = SUBLANES - (SSD_CONV - 1) + j
        acc = acc + cw_ref[j:j + 1, :] * xpad_scr[lo:lo + L, :]
    xc = _silu(acc)
    xs = xc[:, :BRANCH_WIDTH]

    dt = _softplus(_dot_exact(small_ref[0], exp_ref[...]) + dtb_ref[...])
    a = dt * (-jnp.exp(alog_ref[...]))
    rows = lax.broadcasted_iota(jnp.int32, (L, L), 0)
    cols = lax.broadcasted_iota(jnp.int32, (L, L), 1)
    causal = rows >= cols
    acs = _dot_exact(causal.astype(F32), a)
    acs_t = lax.dot_general(sel_ref[...], acs, (((1,), (1,)), ((), ())),
                            precision=HIGHEST, preferred_element_type=F32)
    xdt = xs * dt

    for g in range(SSD_GROUPS):
        b_g = xc[:, BRANCH_WIDTH + g * SSD_STATE:BRANCH_WIDTH + (g + 1) * SSD_STATE].astype(BF16)
        c_g = xc[:, BRANCH_WIDTH + (SSD_GROUPS + g) * SSD_STATE:
                 BRANCH_WIDTH + (SSD_GROUPS + g + 1) * SSD_STATE].astype(BF16)
        cb = lax.dot_general(c_g, b_g, (((1,), (1,)), ((), ())), preferred_element_type=F32)
        for hh in range(N_HEADS // SSD_GROUPS):
            h = g * (N_HEADS // SSD_GROUPS) + hh
            sl = slice(h * HEAD_DIM, (h + 1) * HEAD_DIM)
            col = acs[:, h * HEAD_DIM:h * HEAD_DIM + 1]
            row = acs_t[h:h + 1, :]
            dec = jnp.where(causal, jnp.exp(jnp.minimum(col - row, 0.0)), 0.0)
            x_h = xdt[:, sl]
            st = state_scr[h]
            y_h = _dot((cb * dec).astype(BF16), x_h.astype(BF16))
            y_h = y_h + _dot(c_g, st.astype(BF16)) * jnp.exp(col)
            y_scr[:, sl] = y_h
            last = acs[L - 1:L, h * HEAD_DIM:h * HEAD_DIM + 1]
            x_w = (x_h * jnp.exp(last - col)).astype(BF16)
            state_scr[h] = st * jnp.exp(last) + lax.dot_general(
                b_g, x_w, (((0,), (0,)), ((), ())), preferred_element_type=F32)

    y = y_scr[...] + dsk_ref[...] * xs
    y = y * _silu(z_ref[0])
    half = BRANCH_WIDTH // SSD_GROUPS
    for g in range(SSD_GROUPS):
        yg = y[:, g * half:(g + 1) * half]
        yg = yg * lax.rsqrt(jnp.mean(yg * yg, axis=-1, keepdims=True) + RMS_EPS)
        o_ref[0, :, g * half:(g + 1) * half] = (yg * ng_ref[:, g * half:(g + 1) * half]).astype(o_ref.dtype)


def _ssd(z, xbc, small, conv_w, conv_b, dt_bias, a_log, d_skip, norm_g, L=64):
    b, t, _ = z.shape
    rep = lambda p: jnp.repeat(p.astype(F32), HEAD_DIM).reshape(1, BRANCH_WIDTH)
    blk = lambda n: pl.BlockSpec((1, L, n), lambda i, j: (i, j, 0))
    prev = pl.BlockSpec((1, SUBLANES, SSD_XBC),
                        lambda i, j: (i, jnp.maximum(j * (L // SUBLANES) - 1, 0), 0))
    return pl.pallas_call(
        functools.partial(_ssd_kernel, L=L),
        grid=(b, t // L),
        in_specs=[blk(BRANCH_WIDTH), blk(SSD_XBC), prev, blk(LANES),
                  _resident((SSD_CONV, SSD_XBC)), _resident((1, SSD_XBC)),
                  _resident((1, BRANCH_WIDTH)), _resident((1, BRANCH_WIDTH)), _resident((1, BRANCH_WIDTH)),
                  _resident((1, BRANCH_WIDTH)), _resident((LANES, BRANCH_WIDTH)),
                  _resident((N_HEADS, BRANCH_WIDTH))],
        out_specs=blk(BRANCH_WIDTH),
        out_shape=jax.ShapeDtypeStruct((b, t, BRANCH_WIDTH), BF16),
        scratch_shapes=[pltpu.VMEM((SUBLANES + L, SSD_XBC), F32),
                        pltpu.VMEM((N_HEADS, SSD_STATE, HEAD_DIM), F32),
                        pltpu.VMEM((L, BRANCH_WIDTH), F32)],
        compiler_params=_params("parallel", "arbitrary"),
        name="ssd",
    )(z, xbc, xbc, small, conv_w, conv_b.reshape(1, SSD_XBC), rep(dt_bias), rep(a_log), rep(d_skip),
      norm_g.reshape(1, BRANCH_WIDTH), _head_expand(), _head_select())


S5_CHUNK = 64
S5_ROW = S5_CHUNK * S5_GROUP


def _s5_tables(a_re, a_im, b_re, b_im, c_re, c_im, log_dt):
    L = S5_CHUNK
    f = lambda v: v.astype(F32)
    a_re, a_im, b_re, b_im, c_re, c_im = map(f, (a_re, a_im, b_re, b_im, c_re, c_im))
    dt = jnp.exp(f(log_dt))[:, None]
    lr, li = a_re * dt, a_im * dt
    tau = jnp.arange(L + 1, dtype=F32)[None, :, None]
    mag = jnp.exp(lr[:, None, :] * tau)
    pr, pi = mag * jnp.cos(li[:, None, :] * tau), mag * jnp.sin(li[:, None, :] * tau)
    abr, abi = pr[:, 1], pi[:, 1]
    den = a_re * a_re + a_im * a_im
    qr = ((abr - 1.0) * a_re + abi * a_im) / den
    qi = (abi * a_re - (abr - 1.0) * a_im) / den
    bbr = qr[..., None] * b_re - qi[..., None] * b_im
    bbi = qr[..., None] * b_im + qi[..., None] * b_re
    pbr = pr[..., None] * bbr[:, None] - pi[..., None] * bbi[:, None]
    pbi = pr[..., None] * bbi[:, None] + pi[..., None] * bbr[:, None]
    kern = (jnp.einsum("gop,gtpi->gtoi", c_re, pbr[:, :L], precision=HIGHEST)
            - jnp.einsum("gop,gtpi->gtoi", c_im, pbi[:, :L], precision=HIGHEST))
    tz = _s5_toeplitz(kern.transpose(0, 3, 1, 2).reshape(S5_GROUPS, S5_GROUP, S5_ROW))
    rev = L - 1 - jnp.arange(L)
    ws = jnp.concatenate([pbr[:, rev], pbi[:, rev]], axis=2)
    ws = ws.transpose(0, 1, 3, 2).reshape(S5_GROUPS, S5_ROW, 2 * S5_STATE)
    cpr = c_re[:, None] * pr[:, 1:, None, :] - c_im[:, None] * pi[:, 1:, None, :]
    cpi = c_re[:, None] * pi[:, 1:, None, :] + c_im[:, None] * pr[:, 1:, None, :]
    wy = jnp.concatenate([cpr, -cpi], axis=3).transpose(0, 3, 1, 2).reshape(S5_GROUPS, 2 * S5_STATE, S5_ROW)
    a1 = jnp.concatenate([pr[:, L], pr[:, L]], axis=1)[:, None, :]
    a2 = jnp.concatenate([-pi[:, L], pi[:, L]], axis=1)[:, None, :]
    return tz, ws.astype(BF16), wy.astype(BF16), a1, a2


def _s5_toeplitz_kernel(k_ref, o_ref):
    kf = k_ref[0]
    col = lax.broadcasted_iota(jnp.int32, kf.shape, 1)
    for s in range(S5_CHUNK):
        shifted = pltpu.roll(kf, s * S5_GROUP, axis=1) if s else kf
        o_ref[0, s * S5_GROUP:(s + 1) * S5_GROUP, :] = jnp.where(col >= s * S5_GROUP, shifted, 0.0).astype(BF16)


def _s5_toeplitz(kflat):
    return pl.pallas_call(
        _s5_toeplitz_kernel,
        grid=(S5_GROUPS,),
        in_specs=[pl.BlockSpec((1, S5_GROUP, S5_ROW), lambda g: (g, 0, 0))],
        out_specs=pl.BlockSpec((1, S5_ROW, S5_ROW), lambda g: (g, 0, 0)),
        out_shape=jax.ShapeDtypeStruct((S5_GROUPS, S5_ROW, S5_ROW), BF16),
        compiler_params=_params("parallel"),
        name="s5_toeplitz",
    )(kflat)


def _s5_kernel(u_ref, tz_ref, ws_ref, wy_ref, a1_ref, a2_ref, y_ref, sl_scr, x_scr, *, nb, nc):
    u = u_ref[0]
    y_local = _dot(u, tz_ref[0])
    sl_scr[...] = _dot(u, ws_ref[0])
    a1, a2 = a1_ref[0], a2_ref[0]

    def step(c, x):
        r = pl.ds(pl.multiple_of(c * nb, SUBLANES), nb)
        x_scr[r, :] = x
        return a1 * x + a2 * pltpu.roll(x, S5_STATE, axis=1) + sl_scr[r, :]

    lax.fori_loop(0, nc, step, jnp.zeros((nb, 2 * S5_STATE), F32))
    y_ref[0] = y_local + _dot(x_scr[...].astype(BF16), wy_ref[0])


def _s5_out_kernel(y_ref, u_ref, d_ref, w_ref, b_ref, o_ref):
    y = y_ref[...] + d_ref[...] * u_ref[...]
    y = 0.5 * y * (1.0 + jnp.tanh(math.sqrt(2.0 / math.pi) * (y + 0.044715 * (y * y * y))))
    o_ref[...] = (y * _sigmoid(_dot(y.astype(BF16), w_ref[...]) + b_ref[...])).astype(o_ref.dtype)


def _s5(u, tables, d_skip, glu_w, glu_b, tm=512):
    b, t, _ = u.shape
    nc = t // S5_CHUNK
    m = nc * b
    tz, ws, wy, a1, a2 = tables
    ug = u.reshape(b, nc, S5_CHUNK, S5_GROUPS, S5_GROUP).transpose(3, 1, 0, 2, 4)
    ug = ug.reshape(S5_GROUPS, m, S5_ROW).astype(BF16)
    grp = lambda r, c: pl.BlockSpec((1, r, c), lambda g: (g, 0, 0))
    y = pl.pallas_call(
        functools.partial(_s5_kernel, nb=b, nc=nc),
        grid=(S5_GROUPS,),
        in_specs=[grp(m, S5_ROW), grp(S5_ROW, S5_ROW), grp(S5_ROW, 2 * S5_STATE),
                  grp(2 * S5_STATE, S5_ROW), grp(1, 2 * S5_STATE), grp(1, 2 * S5_STATE)],
        out_specs=grp(m, S5_ROW),
        out_shape=jax.ShapeDtypeStruct((S5_GROUPS, m, S5_ROW), F32),
        scratch_shapes=[pltpu.VMEM((m, 2 * S5_STATE), F32), pltpu.VMEM((m, 2 * S5_STATE), F32)],
        compiler_params=_params("parallel"),
        name="s5_scan",
    )(ug, tz, ws, wy, a1, a2)
    y = y.reshape(S5_GROUPS, nc, b, S5_CHUNK, S5_GROUP).transpose(2, 1, 3, 0, 4).reshape(b * t, BRANCH_WIDTH)
    tm = min(tm, b * t)
    row = pl.BlockSpec((tm, BRANCH_WIDTH), lambda i: (i, 0))
    return pl.pallas_call(
        _s5_out_kernel,
        grid=(b * t // tm,),
        in_specs=[row, row, _resident((1, BRANCH_WIDTH)), _resident((BRANCH_WIDTH, BRANCH_WIDTH)),
                  _resident((1, BRANCH_WIDTH))],
        out_specs=row,
        out_shape=jax.ShapeDtypeStruct((b * t, BRANCH_WIDTH), BF16),
        compiler_params=_params("parallel"),
        name="s5_out",
    )(y, u.reshape(b * t, BRANCH_WIDTH), d_skip.reshape(1, BRANCH_WIDTH).astype(F32), glu_w.astype(BF16),
      glu_b.reshape(1, BRANCH_WIDTH).astype(F32))


RWKV_LORA_OFF = 3 * BRANCH_WIDTH
RWKV_GATE_OFF = RWKV_LORA_OFF + LANES


def _rwkv_pre_kernel(p_ref, prev_ref, mu_ref, w0_ref, w2_ref, a0_ref, a2_ref, g2_ref, kk_ref, ka_ref,
                     r_o, w_o, k_o, v_o, kk_o, a_o, g_o, pad_scr, *, tm):
    j = pl.program_id(1)
    pad_scr[0:SUBLANES, :] = jnp.where(j == 0, 0.0, prev_ref[0])
    pad_scr[SUBLANES:SUBLANES + tm, :] = p_ref[0]
    p = p_ref[0]
    p = p + (pad_scr[SUBLANES - 1:SUBLANES - 1 + tm, :] - p) * mu_ref[...]
    r = p[:, 0:BRANCH_WIDTH]
    k = p[:, BRANCH_WIDTH:2 * BRANCH_WIDTH]
    v = p[:, 2 * BRANCH_WIDTH:3 * BRANCH_WIDTH]
    lora = p[:, RWKV_LORA_OFF:RWKV_GATE_OFF]
    dg = p[:, RWKV_GATE_OFF:RWKV_COLS]
    w = -_softplus(-(w0_ref[...] + _dot(jnp.tanh(lora).astype(BF16), w2_ref[...]))) - 0.5
    a = _sigmoid(a0_ref[...] + _dot(lora.astype(BF16), a2_ref[...]))
    r_o[0] = r
    w_o[0] = jnp.exp(-jnp.exp(w))
    k_o[0] = k * (1.0 + (a - 1.0) * ka_ref[...])
    v_o[0] = v
    kk_o[0] = k * kk_ref[...]
    a_o[0] = a
    g_o[0] = _dot(_sigmoid(dg).astype(BF16), g2_ref[...])


def _rwkv_scan_kernel(r_ref, w_ref, k_ref, v_ref, kk_ref, a_ref, rk_ref, lng_ref, lnb_ref, y_ref,
                      s_scr, even_scr, odd_scr, *, tb):
    nl = s_scr.shape[2]

    @pl.when(pl.program_id(0) == 0)
    def _():
        s_scr[...] = jnp.zeros_like(s_scr)

    def stage(t, buf):
        tile = lambda ref: ref[:, t].reshape(nl, HEAD_DIM).T
        kk = tile(kk_ref)
        kk = kk / jnp.maximum(jnp.sqrt(jnp.sum(kk * kk, axis=0, keepdims=True)), 1e-12)
        for i, x in enumerate((tile(r_ref), tile(w_ref), tile(k_ref), tile(v_ref), -kk, kk * tile(a_ref))):
            buf[i] = x

    def step(t, buf):
        sa = jnp.zeros(s_scr.shape[1:], F32)
        for kx in range(HEAD_DIM):
            sa = sa + s_scr[kx] * buf[4, kx:kx + 1, :]
        v_t = buf[3]
        y = jnp.zeros(s_scr.shape[1:], F32)
        for kx in range(HEAD_DIM):
            s_new = (s_scr[kx] * buf[1, kx:kx + 1, :] + sa * buf[5, kx:kx + 1, :] + v_t * buf[2, kx:kx + 1, :])
            s_scr[kx] = s_new
            y = y + s_new * buf[0, kx:kx + 1, :]
        mean = jnp.mean(y, axis=0, keepdims=True)
        var = jnp.mean(jnp.square(y - mean), axis=0, keepdims=True)
        y = (y - mean) * lax.rsqrt(var + RWKV_GN_EPS) * lng_ref[0] + lnb_ref[0]
        y = y + jnp.sum(buf[0] * buf[2] * rk_ref[0], axis=0, keepdims=True) * v_t
        y_ref[:, t] = y.T.reshape(nl // N_HEADS, N_HEADS, HEAD_DIM)

    def two_steps(i, carry):
        t = 2 * i
        stage(t + 1, odd_scr)
        step(t, even_scr)
        stage(jnp.minimum(t + 2, tb - 1), even_scr)
        step(t + 1, odd_scr)
        return carry

    stage(0, even_scr)
    lax.fori_loop(0, tb // 2, two_steps, 0)


def _rwkv(cols, mu, w0, w2, a0, a2, g2, k_k, k_a, r_k, ln_g, ln_b, tm=256, tb=32):
    b, t, _ = cols.shape
    tm, tb = min(tm, t), min(tb, t)
    vec = lambda p: p.reshape(1, -1).astype(F32)
    zpad = jnp.zeros((LANES // 2, BRANCH_WIDTH), F32)
    blk = lambda n: pl.BlockSpec((1, tm, n), lambda i, j: (i, j, 0))
    prev = pl.BlockSpec((1, SUBLANES, RWKV_COLS),
                        lambda i, j: (i, jnp.maximum(j * (tm // SUBLANES) - 1, 0), 0))
    outs = pl.pallas_call(
        functools.partial(_rwkv_pre_kernel, tm=tm),
        grid=(b, t // tm),
        in_specs=[blk(RWKV_COLS), prev, _resident((1, RWKV_COLS)), _resident((1, BRANCH_WIDTH)),
                  _resident((LANES, BRANCH_WIDTH)), _resident((1, BRANCH_WIDTH)),
                  _resident((LANES, BRANCH_WIDTH)), _resident((LANES, BRANCH_WIDTH)),
                  _resident((1, BRANCH_WIDTH)), _resident((1, BRANCH_WIDTH))],
        out_specs=[blk(BRANCH_WIDTH)] * 7,
        out_shape=[jax.ShapeDtypeStruct((b, t, BRANCH_WIDTH), F32)] * 7,
        scratch_shapes=[pltpu.VMEM((SUBLANES + tm, RWKV_COLS), F32)],
        compiler_params=_params("parallel", "arbitrary"),
        name="rwkv_pre",
    )(cols, cols, vec(mu), vec(w0), jnp.concatenate([w2.astype(F32), zpad]).astype(BF16), vec(a0),
      jnp.concatenate([zpad, a2.astype(F32)]).astype(BF16), g2.astype(BF16), vec(k_k), vec(k_a))
    *seqs, gate = outs
    per_head = lambda p: jnp.tile(p.astype(F32).reshape(N_HEADS, HEAD_DIM).T, (1, b))[None]
    nl = b * N_HEADS
    heads = lambda s: s.reshape(b, t, N_HEADS, HEAD_DIM)
    tblk = pl.BlockSpec((b, tb, N_HEADS, HEAD_DIM), lambda i: (0, i, 0, 0))
    tiles = pltpu.VMEM((6, HEAD_DIM, nl), F32)
    y = pl.pallas_call(
        functools.partial(_rwkv_scan_kernel, tb=tb),
        grid=(t // tb,),
        in_specs=[tblk] * 6 + [_resident((1, HEAD_DIM, nl))] * 3,
        out_specs=tblk,
        out_shape=jax.ShapeDtypeStruct((b, t, N_HEADS, HEAD_DIM), F32),
        scratch_shapes=[pltpu.VMEM((HEAD_DIM, HEAD_DIM, nl), F32), tiles, tiles],
        compiler_params=_params("arbitrary"),
        name="rwkv_scan",
    )(*[heads(s) for s in seqs], per_head(r_k), per_head(ln_g), per_head(ln_b))
    return y.reshape(b * t, BRANCH_WIDTH), gate.reshape(b * t, BRANCH_WIDTH)


DSA_TQ = 128
DSA_TK = 512
INT_MIN = -2 ** 31
MASKED = -1e30


def _bias_tiles():
    n = np.arange(1, 8192, dtype=np.float32)
    large = 8 + (np.log(n / np.float32(8)) / np.float32(math.log(REL_MAX_DIST / 8)) * np.float32(8)).astype(np.int32)
    n_sat = int(n[np.argmax(large >= 15)])
    return -(-(n_sat + DSA_TQ - 1) // DSA_TQ) + 1


def _bias_kernel(rb_ref, o_ref):
    e = pl.program_id(0)
    t = DSA_TQ
    rel = (lax.broadcasted_iota(jnp.int32, (t, t), 0) - lax.broadcasted_iota(jnp.int32, (t, t), 1)) - e * t
    half, exact = REL_BUCKETS // 2, REL_BUCKETS // 4
    n = jnp.abs(rel)
    n_f = jnp.maximum(n, 1).astype(F32)
    large = exact + (jnp.log(n_f / exact) / math.log(REL_MAX_DIST / exact) * (half - exact)).astype(jnp.int32)
    large = jnp.minimum(large, half - 1)
    bucket = jnp.where(rel > 0, half, 0) + jnp.where(n < exact, n, large)
    for h in range(N_HEADS):
        acc = jnp.zeros((t, t), F32)
        for bkt in range(REL_BUCKETS):
            acc = jnp.where(bucket == bkt, rb_ref[bkt, h], acc)
        o_ref[h // 2, 0, :, (h % 2) * t:(h % 2 + 1) * t] = acc


def _rel_bias_table(rel_bias):
    nd = _bias_tiles()
    t = DSA_TQ
    return pl.pallas_call(
        _bias_kernel,
        grid=(nd,),
        in_specs=[pl.BlockSpec(memory_space=pltpu.SMEM)],
        out_specs=pl.BlockSpec((N_HEADS // 2, 1, t, 2 * t), lambda e: (0, e, 0, 0)),
        out_shape=jax.ShapeDtypeStruct((N_HEADS // 2, nd, t, 2 * t), F32),
        compiler_params=_params("parallel"),
        name="rel_bias",
    )(rel_bias.astype(F32))


def _nt_dot(a, b, **kw):
    return lax.dot_general(a, b, (((1,), (1,)), ((), ())), preferred_element_type=F32, **kw)


def _dsa_kernel(q_ref, k_ref, vt_ref, qi_ref, kidx_ref, small_ref, bias_ref, tri_ref, pick_ref, o_ref,
                key_scr, msk_scr, lga_scr, lgb_scr, *, topk, nd):
    tq, tk = DSA_TQ, DSA_TK
    th = tk // 2
    q0 = pl.program_id(1) * tq
    nkb = (q0 + tq + tk - 1) // tk
    int_min = jnp.int32(INT_MIN)
    q_pos = q0 + lax.broadcasted_iota(jnp.int32, (1, tq), 1)
    lim = (q_pos // CHUNK + 1) * CHUNK
    lane = lax.broadcasted_iota(jnp.int32, (tq, LANES), 1)

    qi = qi_ref[0]
    qi_all = jnp.concatenate(
        [jnp.where((lane >= h * IDX_DIM) & (lane < (h + 1) * IDX_DIM), qi, 0.0).astype(BF16)
         for h in range(IDX_HEADS)], axis=0)
    w_t = _nt_dot(pick_ref[...], small_ref[0], precision=HIGHEST)
    w_h = [w_t[h:h + 1, :] for h in range(IDX_HEADS)]

    def score_block(j, carry):
        s0 = pl.multiple_of(j * tk, tk)
        kidx = kidx_ref[0, pl.ds(s0, tk), :]
        dots = _nt_dot(kidx, qi_all)
        s = None
        for h in range(IDX_HEADS):
            term = w_h[h] * jnp.maximum(dots[:, h * tq:(h + 1) * tq], 0.0)
            s = term if s is None else s + term
        s = s + 0.0
        bits = lax.bitcast_convert_type(s, jnp.int32)
        bits = jnp.where(bits < 0, bits ^ jnp.int32(0x7FFFFFFF), bits)
        pos = s0 + lax.broadcasted_iota(jnp.int32, (tk, tq), 0)
        key_scr[pl.ds(s0, tk), :] = jnp.where(pos < lim, bits, int_min)
        return carry

    lax.fori_loop(0, nkb, score_block, 0)

    acc_rows = 4 * SUBLANES

    def count(pred):
        def body(j, acc):
            hit = jnp.where(pred(key_scr[pl.ds(pl.multiple_of(j * tk, tk), tk), :]), 1.0, 0.0)
            return acc + jnp.sum(hit.reshape(tk // acc_rows, acc_rows, tq), axis=0)
        acc = lax.fori_loop(0, nkb, body, jnp.zeros((acc_rows, tq), F32))
        return jnp.sum(acc, axis=0, keepdims=True)

    kf = jnp.float32(topk)
    c0 = count(lambda kb: kb >= 0)
    thr = jnp.where(c0 >= kf, jnp.int32(0), int_min)
    n_ge = jnp.where(c0 >= kf, c0, 0.0)

    def search(it, carry):
        thr, n_ge = carry
        cand = thr + jnp.left_shift(jnp.int32(1), 30 - it)
        c = count(lambda kb: kb >= cand)
        return jnp.where(c >= kf, cand, thr), jnp.where(c >= kf, c, n_ge)

    thr, n_ge = lax.fori_loop(0, 31, search, (thr, n_ge))
    real = thr > int_min
    surplus = jnp.max(jnp.where(real, n_ge - kf, 0.0))

    def select_plain(j, carry):
        s0 = pl.multiple_of(j * tk, tk)
        kb = key_scr[pl.ds(s0, tk), :]
        msk_scr[pl.ds(s0, tk), :] = jnp.where((kb >= thr) & (kb > int_min), 0.0, MASKED)
        return carry

    @pl.when(surplus <= 0.0)
    def _():
        lax.fori_loop(0, nkb, select_plain, 0)

    @pl.when(surplus > 0.0)
    def _():
        quota = kf - count(lambda kb: kb > thr)

        def select_ranked(j, run):
            s0 = pl.multiple_of(j * tk, tk)
            kb = key_scr[pl.ds(s0, tk), :]
            eq = kb == thr
            rank = run + _dot(tri_ref[...], jnp.where(eq, 1.0, 0.0).astype(BF16))
            sel = (kb > thr) | (eq & real & (rank <= quota))
            msk_scr[pl.ds(s0, tk), :] = jnp.where(sel, 0.0, MASKED)
            return rank[tk - 1:tk, :]

        lax.fori_loop(0, nkb, select_ranked, jnp.zeros((1, tq), F32))

    q = q_ref[0]
    dim = lax.broadcasted_iota(jnp.int32, (LANES, tq), 0)
    for hp in range(N_HEADS // 2):
        q_pair = q[:, hp * LANES:(hp + 1) * LANES] * HEAD_DIM ** -0.5
        q2 = jnp.concatenate([jnp.where(lane < HEAD_DIM, q_pair, 0.0).astype(BF16),
                              jnp.where(lane >= HEAD_DIM, q_pair, 0.0).astype(BF16)], axis=0)

        def raw_logits(jh, hp=hp, q2=q2):
            s0 = pl.multiple_of(jh * th, th)
            return _nt_dot(k_ref[0, pl.ds(s0, th), hp * LANES:(hp + 1) * LANES], q2)

        def finish_logits(lg, jh, buf, hp=hp):
            s0 = pl.multiple_of(jh * th, th)
            e0 = (q0 - s0) // tq
            top = None
            for c in range(th // LANES):
                mask = msk_scr[pl.ds(s0 + c * LANES, LANES), :]
                lg_c = (lg[c * LANES:(c + 1) * LANES] + bias_ref[hp, jnp.clip(e0 - c, 0, nd - 1)]
                        + jnp.concatenate([mask, mask], axis=1))
                buf[c * LANES:(c + 1) * LANES, :] = lg_c
                top_c = jnp.max(lg_c, axis=0, keepdims=True)
                top = top_c if top is None else jnp.maximum(top, top_c)
            return top

        def softmax_pv(state, buf, top, j, half, hp=hp):
            m, l, acc = state
            m_new = jnp.maximum(m, top)
            alpha = jnp.exp(m - m_new)
            p = jnp.exp(buf[...] - m_new)
            l = alpha * l + jnp.sum(p, axis=0, keepdims=True)
            vt_blk = vt_ref[0, j, hp * LANES:(hp + 1) * LANES, half * th:(half + 1) * th]
            return m_new, l, alpha * acc + _dot(vt_blk, p.astype(BF16))

        def attend(j, carry):
            *state, top_a, top_b = carry
            ja = jnp.minimum(2 * j + 2, 2 * nkb - 2)
            lg_a, lg_b = raw_logits(ja), raw_logits(ja + 1)
            state = softmax_pv(state, lga_scr, top_a, j, 0)
            state = softmax_pv(state, lgb_scr, top_b, j, 1)
            return (*state, finish_logits(lg_a, ja, lga_scr), finish_logits(lg_b, ja + 1, lgb_scr))

        init = (jnp.full((1, 2 * tq), MASKED, F32), jnp.zeros((1, 2 * tq), F32), jnp.zeros((LANES, 2 * tq), F32),
                finish_logits(raw_logits(0), 0, lga_scr), finish_logits(raw_logits(1), 1, lgb_scr))
        _, l, acc, _, _ = lax.fori_loop(0, nkb, attend, init)
        o2 = acc / l
        out_t = jnp.where(dim < HEAD_DIM, o2[:, :tq], o2[:, tq:])
        o_ref[0, :, hp * LANES:(hp + 1) * LANES] = out_t.T.astype(o_ref.dtype)


def _dsa(q, k, v, qi, kidx, small, bias_tab):
    b, t, _ = q.shape
    topk = min(DSA_MAX_TOPK, t // 4)
    nd = bias_tab.shape[1]
    nblk = t // DSA_TK
    vt = v.reshape(b, nblk, DSA_TK, BRANCH_WIDTH).transpose(0, 1, 3, 2)
    tri = jnp.asarray(np.tril(np.ones((DSA_TK, DSA_TK), np.float32))).astype(BF16)
    pick = np.zeros((SUBLANES, LANES), np.float32)
    pick[np.arange(IDX_HEADS), 40 + np.arange(IDX_HEADS)] = 1.0
    qblk = lambda n: pl.BlockSpec((1, DSA_TQ, n), lambda i, j: (i, j, 0))
    full = lambda *s: pl.BlockSpec((1,) + s, lambda i, j: (i,) + (0,) * len(s))
    return pl.pallas_call(
        functools.partial(_dsa_kernel, topk=topk, nd=nd),
        grid=(b, t // DSA_TQ),
        in_specs=[qblk(BRANCH_WIDTH), full(t, BRANCH_WIDTH), full(nblk, BRANCH_WIDTH, DSA_TK), qblk(LANES),
                  full(t, LANES), qblk(LANES), _resident(bias_tab.shape), _resident((DSA_TK, DSA_TK)),
                  _resident((SUBLANES, LANES))],
        out_specs=qblk(BRANCH_WIDTH),
        out_shape=jax.ShapeDtypeStruct((b, t, BRANCH_WIDTH), BF16),
        scratch_shapes=[pltpu.VMEM((t, DSA_TQ), jnp.int32), pltpu.VMEM((t, DSA_TQ), F32),
                        pltpu.VMEM((DSA_TK // 2, 2 * DSA_TQ), F32), pltpu.VMEM((DSA_TK // 2, 2 * DSA_TQ), F32)],
        compiler_params=_params("parallel", "arbitrary"),
        name="dsa",
    )(q, k, vt, qi, kidx, small, bias_tab, tri, jnp.asarray(pick))


def kernel(x, rel_bias, norm_mix_g, w_in, ssd_conv_w, ssd_conv_b, ssd_dt_bias, ssd_a_log, ssd_d, ssd_norm_g,
           rwkv_mu, rwkv_w0, rwkv_w2, rwkv_a0, rwkv_a2, rwkv_g2, rwkv_k_k, rwkv_k_a, rwkv_r_k, rwkv_ln_g,
           rwkv_ln_b, s5_a_re, s5_a_im, s5_b_re, s5_b_im, s5_c_re, s5_c_im, s5_d, s5_log_dt, s5_glu_w,
           s5_glu_b, w_branch, w_out, norm_ffn_g, ffn_w1, ffn_w3, ffn_w2, norm_final_g):
    b, t, _ = x.shape
    x2 = x.reshape(b * t, D_MODEL).astype(F32)
    seq = lambda a: a.reshape(b, t, a.shape[-1])
    bias_tab = _rel_bias_table(rel_bias)
    for i in range(DEPTH):
        z, xbc, q, k, v, qi, small, kidx, rwkv_cols, s5_u, gates = _proj(x2, norm_mix_g[i], _pack_w_in(w_in[i]))
        small = seq(small)
        y_a = _ssd(seq(z), seq(xbc), small, ssd_conv_w[i], ssd_conv_b[i], ssd_dt_bias[i], ssd_a_log[i],
                   ssd_d[i], ssd_norm_g[i])
        y_b = _dsa(seq(q), seq(k), seq(v), seq(qi), seq(kidx), small, bias_tab)
        y_c, g_c = _rwkv(seq(rwkv_cols), rwkv_mu[i], rwkv_w0[i], rwkv_w2[i], rwkv_a0[i], rwkv_a2[i],
                         rwkv_g2[i], rwkv_k_k[i], rwkv_k_a[i], rwkv_r_k[i], rwkv_ln_g[i], rwkv_ln_b[i])
        tables = _s5_tables(s5_a_re[i], s5_a_im[i], s5_b_re[i], s5_b_im[i], s5_c_re[i], s5_c_im[i],
                            s5_log_dt[i])
        y_d = _s5(seq(s5_u), tables, s5_d[i], s5_glu_w[i], s5_glu_b[i])
        x2 = _merge(x2, y_a.reshape(b * t, BRANCH_WIDTH), y_b.reshape(b * t, BRANCH_WIDTH), y_c, g_c, y_d,
                    gates, w_branch[i].astype(BF16), w_out[i].astype(BF16))
        x2 = _ffn(x2, norm_ffn_g[i], ffn_w1[i].astype(BF16), ffn_w3[i].astype(BF16), ffn_w2[i].astype(BF16),
                  norm_final_g, final_norm=(i == DEPTH - 1))
    return x2.reshape(b, t, D_MODEL)
```

```python
import functools
import math

import jax
import jax.numpy as jnp
import numpy as np
from jax import lax
from jax.experimental import pallas as pl
from jax.experimental.pallas import tpu as pltpu

D_MODEL = 1024
DEPTH = 4
CHUNK = 64
N_BRANCH = 4
BRANCH_WIDTH = 512
HEAD_DIM = 64
N_HEADS = 8
RMS_EPS = 1e-6

SSD_GROUPS = 2
SSD_STATE = 64
SSD_CONV = 4
SSD_XBC = 768

IDX_HEADS = 4
IDX_DIM = 32
DSA_MAX_TOPK = 256
REL_BUCKETS = 32
REL_MAX_DIST = 256

RWKV_COLS = 1792
RWKV_GN_EPS = 64e-5

S5_GROUP = 16
S5_GROUPS = 32
S5_STATE = 64

FFN_HIDDEN = 2816

LANES = 128
SUBLANES = 8
VMEM_LIMIT = 56 * 1024 * 1024

F32 = jnp.float32
BF16 = jnp.bfloat16
HIGHEST = lax.Precision.HIGHEST


def _params(*sem):
    return pltpu.CompilerParams(dimension_semantics=sem, vmem_limit_bytes=VMEM_LIMIT)


def _resident(shape):
    nd = len(shape)
    return pl.BlockSpec(shape, lambda *_: (0,) * nd, pipeline_mode=pl.Buffered(1))


def _dot(a, b):
    return jnp.dot(a, b, preferred_element_type=F32)


def _dot_exact(a, b):
    return jnp.dot(a, b, preferred_element_type=F32, precision=HIGHEST)


def _sigmoid(x):
    return 1.0 / (1.0 + jnp.exp(-x))


def _silu(x):
    return x * _sigmoid(x)


def _softplus(x):
    return jnp.maximum(x, 0.0) + jnp.log(1.0 + jnp.exp(-jnp.abs(x)))


PROJ_SEGS = (
    ("z", 512, F32), ("xbc", 768, F32), ("q", 512, BF16), ("k", 512, BF16), ("v", 512, BF16),
    ("qi", 128, F32), ("small", 128, F32), ("kidx", 128, BF16), ("rwkv", RWKV_COLS, F32), ("s5", 512, F32),
    ("gates", 4096, F32),
)
PROJ_COLS = sum(s[1] for s in PROJ_SEGS)


def _pack_w_in(w):
    o = np.cumsum((0, 512, 768, 8, 512, 512, 512, 128, 32, 4, RWKV_COLS, 512, 4096))
    seg = lambda i: w[:, o[i]:o[i + 1]]
    small = jnp.concatenate([seg(2), seg(7), seg(8), jnp.zeros((D_MODEL, LANES - 44), w.dtype)], axis=1)
    kidx = jnp.tile(seg(7), (1, IDX_HEADS))
    parts = [seg(0), seg(1), seg(3), seg(4), seg(5), seg(6), small, kidx, seg(9), seg(10), seg(11)]
    return jnp.concatenate(parts, axis=1).astype(BF16)


def _proj_kernel(x_ref, g_ref, w_ref, *out_refs):
    x = x_ref[...]
    ms = jnp.mean(x * x, axis=-1, keepdims=True)
    h = (x * lax.rsqrt(ms + RMS_EPS) * g_ref[...]).astype(BF16)
    off = 0
    for o_ref, (_, n, _) in zip(out_refs, PROJ_SEGS):
        o_ref[...] = _dot(h, w_ref[:, off:off + n]).astype(o_ref.dtype)
        off += n


def _proj(x2, g, w_packed, tm=256):
    m = x2.shape[0]
    tm = min(tm, m)
    return pl.pallas_call(
        _proj_kernel,
        grid=(m // tm,),
        in_specs=[pl.BlockSpec((tm, D_MODEL), lambda i: (i, 0)),
                  _resident((1, D_MODEL)), _resident((D_MODEL, PROJ_COLS))],
        out_specs=[pl.BlockSpec((tm, n), lambda i: (i, 0)) for _, n, _ in PROJ_SEGS],
        out_shape=[jax.ShapeDtypeStruct((m, n), dt) for _, n, dt in PROJ_SEGS],
        compiler_params=_params("parallel"),
        name="proj",
    )(x2, g.reshape(1, D_MODEL), w_packed)


def _merge_kernel(x_ref, ya_ref, yb_ref, yc_ref, gc_ref, yd_ref, gate_ref, wb_ref, wo_ref, o_ref):
    ys = (ya_ref[...], yb_ref[...], (yc_ref[...] * gc_ref[...]).astype(BF16), yd_ref[...])
    merged = None
    for n, y in enumerate(ys):
        gate = _sigmoid(gate_ref[:, n * D_MODEL:(n + 1) * D_MODEL])
        term = gate * _dot(y, wb_ref[n])
        merged = term if merged is None else merged + term
    o_ref[...] = x_ref[...] + _dot(merged.astype(BF16), wo_ref[...])


def _merge(x2, y_a, y_b, y_c, g_c, y_d, gates, wb, wo, tm=256):
    m = x2.shape[0]
    tm = min(tm, m)
    row = lambda n: pl.BlockSpec((tm, n), lambda i: (i, 0))
    return pl.pallas_call(
        _merge_kernel,
        grid=(m // tm,),
        in_specs=[row(D_MODEL)] + [row(BRANCH_WIDTH)] * 5 + [row(N_BRANCH * D_MODEL),
                  _resident((N_BRANCH, BRANCH_WIDTH, D_MODEL)), _resident((D_MODEL, D_MODEL))],
        out_specs=row(D_MODEL),
        out_shape=jax.ShapeDtypeStruct((m, D_MODEL), F32),
        compiler_params=_params("parallel"),
        name="merge",
    )(x2, y_a, y_b, y_c, g_c, y_d, gates, wb, wo)


def _ffn_kernel(x_ref, g_ref, w1_ref, w3_ref, w2_ref, gf_ref, o_ref, *, final_norm):
    x = x_ref[...]
    ms = jnp.mean(x * x, axis=-1, keepdims=True)
    h = (x * lax.rsqrt(ms + RMS_EPS) * g_ref[...]).astype(BF16)
    hid = _silu(_dot(h, w1_ref[...])) * _dot(h, w3_ref[...])
    x = x + _dot(hid.astype(BF16), w2_ref[...])
    if final_norm:
        ms = jnp.mean(x * x, axis=-1, keepdims=True)
        x = x * lax.rsqrt(ms + RMS_EPS) * gf_ref[...]
    o_ref[...] = x


def _ffn(x2, g, w1, w3, w2, g_final, final_norm, tm=256):
    m = x2.shape[0]
    tm = min(tm, m)
    row = pl.BlockSpec((tm, D_MODEL), lambda i: (i, 0))
    return pl.pallas_call(
        functools.partial(_ffn_kernel, final_norm=final_norm),
        grid=(m // tm,),
        in_specs=[row, _resident((1, D_MODEL)), _resident((D_MODEL, FFN_HIDDEN)),
                  _resident((D_MODEL, FFN_HIDDEN)), _resident((FFN_HIDDEN, D_MODEL)),
                  _resident((1, D_MODEL))],
        out_specs=row,
        out_shape=jax.ShapeDtypeStruct((m, D_MODEL), F32),
        compiler_params=_params("parallel"),
        name="ffn",
    )(x2, g.reshape(1, D_MODEL), w1, w3, w2, g_final.reshape(1, D_MODEL))


def _head_expand():
    e = np.zeros((LANES, BRANCH_WIDTH), np.float32)
    for h in range(N_HEADS):
        e[h, h * HEAD_DIM:(h + 1) * HEAD_DIM] = 1.0
    return jnp.asarray(e)


def _head_select():
    s = np.zeros((N_HEADS, BRANCH_WIDTH), np.float32)
    for h in range(N_HEADS):
        s[h, h * HEAD_DIM] = 1.0
    return jnp.asarray(s)


def _ssd_kernel(z_ref, xbc_ref, xprev_ref, small_ref, cw_ref, cb_ref, dtb_ref, alog_ref, dsk_ref,
                ng_ref, exp_ref, sel_ref, o_ref, xpad_scr, state_scr, y_scr, *, L):
    c = pl.program_id(1)

    @pl.when(c == 0)
    def _():
        state_scr[...] = jnp.zeros_like(state_scr)

    xpad_scr[0:SUBLANES, :] = jnp.where(c == 0, 0.0, xprev_ref[0])
    xpad_scr[SUBLANES:SUBLANES + L, :] = xbc_ref[0]
    acc = cb_ref[...]
    for j in range(SSD_CONV):
        lo = SUBLANES - (SSD_CONV - 1) + j
        acc = acc + cw_ref[j:j + 1, :] * xpad_scr[lo:lo + L, :]
    xc = _silu(acc)
    xs = xc[:, :BRANCH_WIDTH]

    dt = _softplus(_dot_exact(small_ref[0], exp_ref[...]) + dtb_ref[...])
    a = dt * (-jnp.exp(alog_ref[...]))
    rows = lax.broadcasted_iota(jnp.int32, (L, L), 0)
    cols = lax.broadcasted_iota(jnp.int32, (L, L), 1)
    causal = rows >= cols
    acs = _dot_exact(causal.astype(F32), a)
    acs_t = lax.dot_general(sel_ref[...], acs, (((1,), (1,)), ((), ())),
                            precision=HIGHEST, preferred_element_type=F32)
    xdt = xs * dt

    for g in range(SSD_GROUPS):
        b_g = xc[:, BRANCH_WIDTH + g * SSD_STATE:BRANCH_WIDTH + (g + 1) * SSD_STATE].astype(BF16)
        c_g = xc[:, BRANCH_WIDTH + (SSD_GROUPS + g) * SSD_STATE:
                 BRANCH_WIDTH + (SSD_GROUPS + g + 1) * SSD_STATE].astype(BF16)
        cb = lax.dot_general(c_g, b_g, (((1,), (1,)), ((), ())), preferred_element_type=F32)
        for hh in range(N_HEADS // SSD_GROUPS):
            h = g * (N_HEADS // SSD_GROUPS) + hh
            sl = slice(h * HEAD_DIM, (h + 1) * HEAD_DIM)
            col = acs[:, h * HEAD_DIM:h * HEAD_DIM + 1]
            row = acs_t[h:h + 1, :]
            dec = jnp.where(causal, jnp.exp(jnp.minimum(col - row, 0.0)), 0.0)
            x_h = xdt[:, sl]
            st = state_scr[h]
            y_h = _dot((cb * dec).astype(BF16), x_h.astype(BF16))
            y_h = y_h + _dot(c_g, st.astype(BF16)) * jnp.exp(col)
            y_scr[:, sl] = y_h
            last = acs[L - 1:L, h * HEAD_DIM:h * HEAD_DIM + 1]
            x_w = (x_h * jnp.exp(last - col)).astype(BF16)
            state_scr[h] = st * jnp.exp(last) + lax.dot_general(
                b_g, x_w, (((0,), (0,)), ((), ())), preferred_element_type=F32)

    y = y_scr[...] + dsk_ref[...] * xs
    y = y * _silu(z_ref[0])
    half = BRANCH_WIDTH // SSD_GROUPS
    for g in range(SSD_GROUPS):
        yg = y[:, g * half:(g + 1) * half]
        yg = yg * lax.rsqrt(jnp.mean(yg * yg, axis=-1, keepdims=True) + RMS_EPS)
        o_ref[0, :, g * half:(g + 1) * half] = (yg * ng_ref[:, g * half:(g + 1) * half]).astype(o_ref.dtype)


def _ssd(z, xbc, small, conv_w, conv_b, dt_bias, a_log, d_skip, norm_g, L=256):
    b, t, _ = z.shape
    rep = lambda p: jnp.repeat(p.astype(F32), HEAD_DIM).reshape(1, BRANCH_WIDTH)
    blk = lambda n: pl.BlockSpec((1, L, n), lambda i, j: (i, j, 0))
    prev = pl.BlockSpec((1, SUBLANES, SSD_XBC),
                        lambda i, j: (i, jnp.maximum(j * (L // SUBLANES) - 1, 0), 0))
    return pl.pallas_call(
        functools.partial(_ssd_kernel, L=L),
        grid=(b, t // L),
        in_specs=[blk(BRANCH_WIDTH), blk(SSD_XBC), prev, blk(LANES),
                  _resident((SSD_CONV, SSD_XBC)), _resident((1, SSD_XBC)),
                  _resident((1, BRANCH_WIDTH)), _resident((1, BRANCH_WIDTH)), _resident((1, BRANCH_WIDTH)),
                  _resident((1, BRANCH_WIDTH)), _resident((LANES, BRANCH_WIDTH)),
                  _resident((N_HEADS, BRANCH_WIDTH))],
        out_specs=blk(BRANCH_WIDTH),
        out_shape=jax.ShapeDtypeStruct((b, t, BRANCH_WIDTH), BF16),
        scratch_shapes=[pltpu.VMEM((SUBLANES + L, SSD_XBC), F32),
                        pltpu.VMEM((N_HEADS, SSD_STATE, HEAD_DIM), F32),
                        pltpu.VMEM((L, BRANCH_WIDTH), F32)],
        compiler_params=_params("parallel", "arbitrary"),
        name="ssd",
    )(z, xbc, xbc, small, conv_w, conv_b.reshape(1, SSD_XBC), rep(dt_bias), rep(a_log), rep(d_skip),
      norm_g.reshape(1, BRANCH_WIDTH), _head_expand(), _head_select())


S5_CHUNK = 64
S5_ROW = S5_CHUNK * S5_GROUP


def _s5_tables(a_re, a_im, b_re, b_im, c_re, c_im, log_dt):
    L = S5_CHUNK
    f = lambda v: v.astype(F32)
    a_re, a_im, b_re, b_im, c_re, c_im = map(f, (a_re, a_im, b_re, b_im, c_re, c_im))
    dt = jnp.exp(f(log_dt))[:, None]
    lr, li = a_re * dt, a_im * dt
    tau = jnp.arange(L + 1, dtype=F32)[None, :, None]
    mag = jnp.exp(lr[:, None, :] * tau)
    pr, pi = mag * jnp.cos(li[:, None, :] * tau), mag * jnp.sin(li[:, None, :] * tau)
    abr, abi = pr[:, 1], pi[:, 1]
    den = a_re * a_re + a_im * a_im
    qr = ((abr - 1.0) * a_re + abi * a_im) / den
    qi = (abi * a_re - (abr - 1.0) * a_im) / den
    bbr = qr[..., None] * b_re - qi[..., None] * b_im
    bbi = qr[..., None] * b_im + qi[..., None] * b_re
    pbr = pr[..., None] * bbr[:, None] - pi[..., None] * bbi[:, None]
    pbi = pr[..., None] * bbi[:, None] + pi[..., None] * bbr[:, None]
    kern = (jnp.einsum("gop,gtpi->gtoi", c_re, pbr[:, :L], precision=HIGHEST)
            - jnp.einsum("gop,gtpi->gtoi", c_im, pbi[:, :L], precision=HIGHEST))
    tz = _s5_toeplitz(kern.transpose(0, 3, 1, 2).reshape(S5_GROUPS, S5_GROUP, S5_ROW))
    rev = L - 1 - jnp.arange(L)
    ws = jnp.concatenate([pbr[:, rev], pbi[:, rev]], axis=2)
    ws = ws.transpose(0, 1, 3, 2).reshape(S5_GROUPS, S5_ROW, 2 * S5_STATE)
    cpr = c_re[:, None] * pr[:, 1:, None, :] - c_im[:, None] * pi[:, 1:, None, :]
    cpi = c_re[:, None] * pi[:, 1:, None, :] + c_im[:, None] * pr[:, 1:, None, :]
    wy = jnp.concatenate([cpr, -cpi], axis=3).transpose(0, 3, 1, 2).reshape(S5_GROUPS, 2 * S5_STATE, S5_ROW)
    a1 = jnp.concatenate([pr[:, L], pr[:, L]], axis=1)[:, None, :]
    a2 = jnp.concatenate([-pi[:, L], pi[:, L]], axis=1)[:, None, :]
    return tz, ws.astype(BF16), wy.astype(BF16), a1, a2


def _s5_toeplitz_kernel(k_ref, o_ref):
    kf = k_ref[0]
    col = lax.broadcasted_iota(jnp.int32, kf.shape, 1)
    for s in range(S5_CHUNK):
        shifted = pltpu.roll(kf, s * S5_GROUP, axis=1) if s else kf
        o_ref[0, s * S5_GROUP:(s + 1) * S5_GROUP, :] = jnp.where(col >= s * S5_GROUP, shifted, 0.0).astype(BF16)


def _s5_toeplitz(kflat):
    return pl.pallas_call(
        _s5_toeplitz_kernel,
        grid=(S5_GROUPS,),
        in_specs=[pl.BlockSpec((1, S5_GROUP, S5_ROW), lambda g: (g, 0, 0))],
        out_specs=pl.BlockSpec((1, S5_ROW, S5_ROW), lambda g: (g, 0, 0)),
        out_shape=jax.ShapeDtypeStruct((S5_GROUPS, S5_ROW, S5_ROW), BF16),
        compiler_params=_params("parallel"),
        name="s5_toeplitz",
    )(kflat)


def _s5_kernel(u_ref, tz_ref, ws_ref, wy_ref, a1_ref, a2_ref, y_ref, sl_scr, x_scr, *, nb, nc):
    u = u_ref[0]
    y_local = _dot(u, tz_ref[0])
    sl_scr[...] = _dot(u, ws_ref[0])
    a1, a2 = a1_ref[0], a2_ref[0]

    def step(c, x):
        r = pl.ds(pl.multiple_of(c * nb, SUBLANES), nb)
        x_scr[r, :] = x
        return a1 * x + a2 * pltpu.roll(x, S5_STATE, axis=1) + sl_scr[r, :]

    lax.fori_loop(0, nc, step, jnp.zeros((nb, 2 * S5_STATE), F32))
    y_ref[0] = y_local + _dot(x_scr[...].astype(BF16), wy_ref[0])


def _s5_out_kernel(y_ref, u_ref, d_ref, w_ref, b_ref, o_ref):
    y = y_ref[...] + d_ref[...] * u_ref[...]
    y = 0.5 * y * (1.0 + jnp.tanh(math.sqrt(2.0 / math.pi) * (y + 0.044715 * (y * y * y))))
    o_ref[...] = (y * _sigmoid(_dot(y.astype(BF16), w_ref[...]) + b_ref[...])).astype(o_ref.dtype)


def _s5(u, tables, d_skip, glu_w, glu_b, tm=512):
    b, t, _ = u.shape
    nc = t // S5_CHUNK
    m = nc * b
    tz, ws, wy, a1, a2 = tables
    ug = u.reshape(b, nc, S5_CHUNK, S5_GROUPS, S5_GROUP).transpose(3, 1, 0, 2, 4)
    ug = ug.reshape(S5_GROUPS, m, S5_ROW).astype(BF16)
    grp = lambda r, c: pl.BlockSpec((1, r, c), lambda g: (g, 0, 0))
    y = pl.pallas_call(
        functools.partial(_s5_kernel, nb=b, nc=nc),
        grid=(S5_GROUPS,),
        in_specs=[grp(m, S5_ROW), grp(S5_ROW, S5_ROW), grp(S5_ROW, 2 * S5_STATE),
                  grp(2 * S5_STATE, S5_ROW), grp(1, 2 * S5_STATE), grp(1, 2 * S5_STATE)],
        out_specs=grp(m, S5_ROW),
        out_shape=jax.ShapeDtypeStruct((S5_GROUPS, m, S5_ROW), F32),
        scratch_shapes=[pltpu.VMEM((m, 2 * S5_STATE), F32), pltpu.VMEM((m, 2 * S5_STATE), F32)],
        compiler_params=_params("parallel"),
        name="s5_scan",
    )(ug, tz, ws, wy, a1, a2)
    y = y.reshape(S5_GROUPS, nc, b, S5_CHUNK, S5_GROUP).transpose(2, 1, 3, 0, 4).reshape(b * t, BRANCH_WIDTH)
    tm = min(tm, b * t)
    row = pl.BlockSpec((tm, BRANCH_WIDTH), lambda i: (i, 0))
    return pl.pallas_call(
        _s5_out_kernel,
        grid=(b * t // tm,),
        in_specs=[row, row, _resident((1, BRANCH_WIDTH)), _resident((BRANCH_WIDTH, BRANCH_WIDTH)),
                  _resident((1, BRANCH_WIDTH))],
        out_specs=row,
        out_shape=jax.ShapeDtypeStruct((b * t, BRANCH_WIDTH), BF16),
        compiler_params=_params("parallel"),
        name="s5_out",
    )(y, u.reshape(b * t, BRANCH_WIDTH), d_skip.reshape(1, BRANCH_WIDTH).astype(F32), glu_w.astype(BF16),
      glu_b.reshape(1, BRANCH_WIDTH).astype(F32))


RWKV_LORA_OFF = 3 * BRANCH_WIDTH
RWKV_GATE_OFF = RWKV_LORA_OFF + LANES


def _rwkv_pre_kernel(p_ref, prev_ref, mu_ref, w0_ref, w2_ref, a0_ref, a2_ref, g2_ref, kk_ref, ka_ref,
                     r_o, w_o, k_o, v_o, kk_o, a_o, g_o, pad_scr, *, tm):
    j = pl.program_id(1)
    pad_scr[0:SUBLANES, :] = jnp.where(j == 0, 0.0, prev_ref[0])
    pad_scr[SUBLANES:SUBLANES + tm, :] = p_ref[0]
    p = p_ref[0]
    p = p + (pad_scr[SUBLANES - 1:SUBLANES - 1 + tm, :] - p) * mu_ref[...]
    r = p[:, 0:BRANCH_WIDTH]
    k = p[:, BRANCH_WIDTH:2 * BRANCH_WIDTH]
    v = p[:, 2 * BRANCH_WIDTH:3 * BRANCH_WIDTH]
    lora = p[:, RWKV_LORA_OFF:RWKV_GATE_OFF]
    dg = p[:, RWKV_GATE_OFF:RWKV_COLS]
    w = -_softplus(-(w0_ref[...] + _dot(jnp.tanh(lora).astype(BF16), w2_ref[...]))) - 0.5
    a = _sigmoid(a0_ref[...] + _dot(lora.astype(BF16), a2_ref[...]))
    r_o[0] = r
    w_o[0] = jnp.exp(-jnp.exp(w))
    k_o[0] = k * (1.0 + (a - 1.0) * ka_ref[...])
    v_o[0] = v
    kk_o[0] = k * kk_ref[...]
    a_o[0] = a
    g_o[0] = _dot(_sigmoid(dg).astype(BF16), g2_ref[...])


def _rwkv_scan_kernel(r_ref, w_ref, k_ref, v_ref, kk_ref, a_ref, rk_ref, lng_ref, lnb_ref, y_ref,
                      s_scr, even_scr, odd_scr, *, tb):
    nl = s_scr.shape[2]

    @pl.when(pl.program_id(0) == 0)
    def _():
        s_scr[...] = jnp.zeros_like(s_scr)

    def stage(t, buf):
        tile = lambda ref: ref[:, t].reshape(nl, HEAD_DIM).T
        kk = tile(kk_ref)
        kk = kk / jnp.maximum(jnp.sqrt(jnp.sum(kk * kk, axis=0, keepdims=True)), 1e-12)
        for i, x in enumerate((tile(r_ref), tile(w_ref), tile(k_ref), tile(v_ref), -kk, kk * tile(a_ref))):
            buf[i] = x

    def step(t, buf):
        row = lambda i, kx: buf[i, pl.ds(kx, 1), :]

        def s_times_a(kb, sa):
            for j in range(SUBLANES):
                kx = kb * SUBLANES + j
                sa = sa + s_scr[kx] * row(4, kx)
            return sa

        sa = jnp.zeros(s_scr.shape[1:], F32)
        for kb in range(HEAD_DIM // SUBLANES):
            sa = s_times_a(kb, sa)
        v_t = buf[3]

        def update(kb, y):
            for j in range(SUBLANES):
                kx = kb * SUBLANES + j
                s_new = s_scr[kx] * row(1, kx) + sa * row(5, kx) + v_t * row(2, kx)
                s_scr[kx] = s_new
                y = y + s_new * row(0, kx)
            return y

        y = lax.fori_loop(0, HEAD_DIM // SUBLANES, update, jnp.zeros(s_scr.shape[1:], F32))
        mean = jnp.mean(y, axis=0, keepdims=True)
        var = jnp.mean(jnp.square(y - mean), axis=0, keepdims=True)
        y = (y - mean) * lax.rsqrt(var + RWKV_GN_EPS) * lng_ref[0] + lnb_ref[0]
        y = y + jnp.sum(buf[0] * buf[2] * rk_ref[0], axis=0, keepdims=True) * v_t
        y_ref[:, t] = y.T.reshape(nl // N_HEADS, N_HEADS, HEAD_DIM)

    def two_steps(i, carry):
        t = 2 * i
        stage(t + 1, odd_scr)
        step(t, even_scr)
        stage(jnp.minimum(t + 2, tb - 1), even_scr)
        step(t + 1, odd_scr)
        return carry

    stage(0, even_scr)
    lax.fori_loop(0, tb // 2, two_steps, 0)


def _rwkv(cols, mu, w0, w2, a0, a2, g2, k_k, k_a, r_k, ln_g, ln_b, tm=256, tb=32):
    b, t, _ = cols.shape
    tm, tb = min(tm, t), min(tb, t)
    vec = lambda p: p.reshape(1, -1).astype(F32)
    zpad = jnp.zeros((LANES // 2, BRANCH_WIDTH), F32)
    blk = lambda n: pl.BlockSpec((1, tm, n), lambda i, j: (i, j, 0))
    prev = pl.BlockSpec((1, SUBLANES, RWKV_COLS),
                        lambda i, j: (i, jnp.maximum(j * (tm // SUBLANES) - 1, 0), 0))
    outs = pl.pallas_call(
        functools.partial(_rwkv_pre_kernel, tm=tm),
        grid=(b, t // tm),
        in_specs=[blk(RWKV_COLS), prev, _resident((1, RWKV_COLS)), _resident((1, BRANCH_WIDTH)),
                  _resident((LANES, BRANCH_WIDTH)), _resident((1, BRANCH_WIDTH)),
                  _resident((LANES, BRANCH_WIDTH)), _resident((LANES, BRANCH_WIDTH)),
                  _resident((1, BRANCH_WIDTH)), _resident((1, BRANCH_WIDTH))],
        out_specs=[blk(BRANCH_WIDTH)] * 7,
        out_shape=[jax.ShapeDtypeStruct((b, t, BRANCH_WIDTH), F32)] * 7,
        scratch_shapes=[pltpu.VMEM((SUBLANES + tm, RWKV_COLS), F32)],
        compiler_params=_params("parallel", "arbitrary"),
        name="rwkv_pre",
    )(cols, cols, vec(mu), vec(w0), jnp.concatenate([w2.astype(F32), zpad]).astype(BF16), vec(a0),
      jnp.concatenate([zpad, a2.astype(F32)]).astype(BF16), g2.astype(BF16), vec(k_k), vec(k_a))
    *seqs, gate = outs
    per_head = lambda p: jnp.tile(p.astype(F32).reshape(N_HEADS, HEAD_DIM).T, (1, b))[None]
    nl = b * N_HEADS
    heads = lambda s: s.reshape(b, t, N_HEADS, HEAD_DIM)
    tblk = pl.BlockSpec((b, tb, N_HEADS, HEAD_DIM), lambda i: (0, i, 0, 0))
    tiles = pltpu.VMEM((6, HEAD_DIM, nl), F32)
    y = pl.pallas_call(
        functools.partial(_rwkv_scan_kernel, tb=tb),
        grid=(t // tb,),
        in_specs=[tblk] * 6 + [_resident((1, HEAD_DIM, nl))] * 3,
        out_specs=tblk,
        out_shape=jax.ShapeDtypeStruct((b, t, N_HEADS, HEAD_DIM), F32),
        scratch_shapes=[pltpu.VMEM((HEAD_DIM, HEAD_DIM, nl), F32), tiles, tiles],
        compiler_params=_params("arbitrary"),
        name="rwkv_scan",
    )(*[heads(s) for s in seqs], per_head(r_k), per_head(ln_g), per_head(ln_b))
    return y.reshape(b * t, BRANCH_WIDTH), gate.reshape(b * t, BRANCH_WIDTH)


DSA_TQ = 128
DSA_TK = 512
INT_MIN = -2 ** 31
MASKED = -1e30


def _bias_tiles():
    n = np.arange(1, 8192, dtype=np.float32)
    large = 8 + (np.log(n / np.float32(8)) / np.float32(math.log(REL_MAX_DIST / 8)) * np.float32(8)).astype(np.int32)
    n_sat = int(n[np.argmax(large >= 15)])
    return -(-(n_sat + DSA_TQ - 1) // DSA_TQ) + 1


def _bias_kernel(rb_ref, o_ref):
    e = pl.program_id(0)
    t = DSA_TQ
    rel = (lax.broadcasted_iota(jnp.int32, (t, t), 0) - lax.broadcasted_iota(jnp.int32, (t, t), 1)) - e * t
    half, exact = REL_BUCKETS // 2, REL_BUCKETS // 4
    n = jnp.abs(rel)
    n_f = jnp.maximum(n, 1).astype(F32)
    large = exact + (jnp.log(n_f / exact) / math.log(REL_MAX_DIST / exact) * (half - exact)).astype(jnp.int32)
    large = jnp.minimum(large, half - 1)
    bucket = jnp.where(rel > 0, half, 0) + jnp.where(n < exact, n, large)
    for h in range(N_HEADS):
        acc = jnp.zeros((t, t), F32)
        for bkt in range(REL_BUCKETS):
            acc = jnp.where(bucket == bkt, rb_ref[bkt, h], acc)
        o_ref[h // 2, 0, :, (h % 2) * t:(h % 2 + 1) * t] = acc


def _rel_bias_table(rel_bias):
    nd = _bias_tiles()
    t = DSA_TQ
    return pl.pallas_call(
        _bias_kernel,
        grid=(nd,),
        in_specs=[pl.BlockSpec(memory_space=pltpu.SMEM)],
        out_specs=pl.BlockSpec((N_HEADS // 2, 1, t, 2 * t), lambda e: (0, e, 0, 0)),
        out_shape=jax.ShapeDtypeStruct((N_HEADS // 2, nd, t, 2 * t), F32),
        compiler_params=_params("parallel"),
        name="rel_bias",
    )(rel_bias.astype(F32))


def _nt_dot(a, b, **kw):
    return lax.dot_general(a, b, (((1,), (1,)), ((), ())), preferred_element_type=F32, **kw)


def _dsa_kernel(q_ref, k_ref, vt_ref, qi_ref, kidx_ref, small_ref, bias_ref, tri_ref, pick_ref, o_ref,
                key_scr, msk_scr, lga_scr, lgb_scr, *, topk, nd):
    tq, tk = DSA_TQ, DSA_TK
    th = tk // 2
    q0 = pl.program_id(1) * tq
    nkb = (q0 + tq + tk - 1) // tk
    int_min = jnp.int32(INT_MIN)
    q_pos = q0 + lax.broadcasted_iota(jnp.int32, (1, tq), 1)
    lim = (q_pos // CHUNK + 1) * CHUNK
    lane = lax.broadcasted_iota(jnp.int32, (tq, LANES), 1)

    qi = qi_ref[0]
    qi_all = jnp.concatenate(
        [jnp.where((lane >= h * IDX_DIM) & (lane < (h + 1) * IDX_DIM), qi, 0.0).astype(BF16)
         for h in range(IDX_HEADS)], axis=0)
    w_t = _nt_dot(pick_ref[...], small_ref[0], precision=HIGHEST)
    w_h = [w_t[h:h + 1, :] for h in range(IDX_HEADS)]

    def score_block(j, carry):
        s0 = pl.multiple_of(j * tk, tk)
        kidx = kidx_ref[0, pl.ds(s0, tk), :]
        dots = _nt_dot(kidx, qi_all)
        s = None
        for h in range(IDX_HEADS):
            term = w_h[h] * jnp.maximum(dots[:, h * tq:(h + 1) * tq], 0.0)
            s = term if s is None else s + term
        s = s + 0.0
        bits = lax.bitcast_convert_type(s, jnp.int32)
        bits = jnp.where(bits < 0, bits ^ jnp.int32(0x7FFFFFFF), bits)
        pos = s0 + lax.broadcasted_iota(jnp.int32, (tk, tq), 0)
        key_scr[pl.ds(s0, tk), :] = jnp.where(pos < lim, bits, int_min)
        return carry

    lax.fori_loop(0, nkb, score_block, 0)

    acc_rows = 4 * SUBLANES

    def count(pred):
        def body(j, acc):
            hit = jnp.where(pred(key_scr[pl.ds(pl.multiple_of(j * tk, tk), tk), :]), 1.0, 0.0)
            return acc + jnp.sum(hit.reshape(tk // acc_rows, acc_rows, tq), axis=0)
        acc = lax.fori_loop(0, nkb, body, jnp.zeros((acc_rows, tq), F32))
        return jnp.sum(acc, axis=0, keepdims=True)

    kf = jnp.float32(topk)
    c0 = count(lambda kb: kb >= 0)
    thr = jnp.where(c0 >= kf, jnp.int32(0), int_min)
    n_ge = jnp.where(c0 >= kf, c0, 0.0)

    def search(it, carry):
        thr, n_ge = carry
        cand = thr + jnp.left_shift(jnp.int32(1), 30 - it)
        c = count(lambda kb: kb >= cand)
        return jnp.where(c >= kf, cand, thr), jnp.where(c >= kf, c, n_ge)

    thr, n_ge = lax.fori_loop(0, 31, search, (thr, n_ge))
    real = thr > int_min
    surplus = jnp.max(jnp.where(real, n_ge - kf, 0.0))

    def select_plain(j, carry):
        s0 = pl.multiple_of(j * tk, tk)
        kb = key_scr[pl.ds(s0, tk), :]
        msk_scr[pl.ds(s0, tk), :] = jnp.where((kb >= thr) & (kb > int_min), 0.0, MASKED)
        return carry

    @pl.when(surplus <= 0.0)
    def _():
        lax.fori_loop(0, nkb, select_plain, 0)

    @pl.when(surplus > 0.0)
    def _():
        quota = kf - count(lambda kb: kb > thr)

        def select_ranked(j, run):
            s0 = pl.multiple_of(j * tk, tk)
            kb = key_scr[pl.ds(s0, tk), :]
            eq = kb == thr
            rank = run + _dot(tri_ref[...], jnp.where(eq, 1.0, 0.0).astype(BF16))
            sel = (kb > thr) | (eq & real & (rank <= quota))
            msk_scr[pl.ds(s0, tk), :] = jnp.where(sel, 0.0, MASKED)
            return rank[tk - 1:tk, :]

        lax.fori_loop(0, nkb, select_ranked, jnp.zeros((1, tq), F32))

    q = q_ref[0]
    dim = lax.broadcasted_iota(jnp.int32, (LANES, tq), 0)
    for hp in range(N_HEADS // 2):
        q_pair = q[:, hp * LANES:(hp + 1) * LANES] * HEAD_DIM ** -0.5
        q2 = jnp.concatenate([jnp.where(lane < HEAD_DIM, q_pair, 0.0).astype(BF16),
                              jnp.where(lane >= HEAD_DIM, q_pair, 0.0).astype(BF16)], axis=0)

        def raw_logits(jh, hp=hp, q2=q2):
            s0 = pl.multiple_of(jh * th, th)
            return _nt_dot(k_ref[0, pl.ds(s0, th), hp * LANES:(hp + 1) * LANES], q2)

        def finish_logits(lg, jh, buf, hp=hp):
            s0 = pl.multiple_of(jh * th, th)
            e0 = (q0 - s0) // tq
            top = None
            for c in range(th // LANES):
                mask = msk_scr[pl.ds(s0 + c * LANES, LANES), :]
                lg_c = (lg[c * LANES:(c + 1) * LANES] + bias_ref[hp, jnp.clip(e0 - c, 0, nd - 1)]
                        + jnp.concatenate([mask, mask], axis=1))
                buf[c * LANES:(c + 1) * LANES, :] = lg_c
                top_c = jnp.max(lg_c, axis=0, keepdims=True)
                top = top_c if top is None else jnp.maximum(top, top_c)
            return top

        def softmax_pv(state, buf, top, j, half, hp=hp):
            m, l, acc = state
            m_new = jnp.maximum(m, top)
            alpha = jnp.exp(m - m_new)
            p = jnp.exp(buf[...] - m_new)
            l = alpha * l + jnp.sum(p, axis=0, keepdims=True)
            vt_blk = vt_ref[0, j, hp * LANES:(hp + 1) * LANES, half * th:(half + 1) * th]
            return m_new, l, alpha * acc + _dot(vt_blk, p.astype(BF16))

        def attend(j, carry):
            *state, top_a, top_b = carry
            ja = jnp.minimum(2 * j + 2, 2 * nkb - 2)
            lg_a, lg_b = raw_logits(ja), raw_logits(ja + 1)
            state = softmax_pv(state, lga_scr, top_a, j, 0)
            state = softmax_pv(state, lgb_scr, top_b, j, 1)
            return (*state, finish_logits(lg_a, ja, lga_scr), finish_logits(lg_b, ja + 1, lgb_scr))

        init = (jnp.full((1, 2 * tq), MASKED, F32), jnp.zeros((1, 2 * tq), F32), jnp.zeros((LANES, 2 * tq), F32),
                finish_logits(raw_logits(0), 0, lga_scr), finish_logits(raw_logits(1), 1, lgb_scr))
        _, l, acc, _, _ = lax.fori_loop(0, nkb, attend, init)
        o2 = acc / l
        out_t = jnp.where(dim < HEAD_DIM, o2[:, :tq], o2[:, tq:])
        o_ref[0, :, hp * LANES:(hp + 1) * LANES] = out_t.T.astype(o_ref.dtype)


def _dsa(q, k, v, qi, kidx, small, bias_tab):
    b, t, _ = q.shape
    topk = min(DSA_MAX_TOPK, t // 4)
    nd = bias_tab.shape[1]
    nblk = t // DSA_TK
    vt = v.reshape(b, nblk, DSA_TK, BRANCH_WIDTH).transpose(0, 1, 3, 2)
    tri = jnp.asarray(np.tril(np.ones((DSA_TK, DSA_TK), np.float32))).astype(BF16)
    pick = np.zeros((SUBLANES, LANES), np.float32)
    pick[np.arange(IDX_HEADS), 40 + np.arange(IDX_HEADS)] = 1.0
    qblk = lambda n: pl.BlockSpec((1, DSA_TQ, n), lambda i, j: (i, j, 0))
    full = lambda *s: pl.BlockSpec((1,) + s, lambda i, j: (i,) + (0,) * len(s))
    return pl.pallas_call(
        functools.partial(_dsa_kernel, topk=topk, nd=nd),
        grid=(b, t // DSA_TQ),
        in_specs=[qblk(BRANCH_WIDTH), full(t, BRANCH_WIDTH), full(nblk, BRANCH_WIDTH, DSA_TK), qblk(LANES),
                  full(t, LANES), qblk(LANES), _resident(bias_tab.shape), _resident((DSA_TK, DSA_TK)),
                  _resident((SUBLANES, LANES))],
        out_specs=qblk(BRANCH_WIDTH),
        out_shape=jax.ShapeDtypeStruct((b, t, BRANCH_WIDTH), BF16),
        scratch_shapes=[pltpu.VMEM((t, DSA_TQ), jnp.int32), pltpu.VMEM((t, DSA_TQ), F32),
                        pltpu.VMEM((DSA_TK // 2, 2 * DSA_TQ), F32), pltpu.VMEM((DSA_TK // 2, 2 * DSA_TQ), F32)],
        compiler_params=_params("parallel", "arbitrary"),
        name="dsa",
    )(q, k, vt, qi, kidx, small, bias_tab, tri, jnp.asarray(pick))


def kernel(x, rel_bias, norm_mix_g, w_in, ssd_conv_w, ssd_conv_b, ssd_dt_bias, ssd_a_log, ssd_d, ssd_norm_g,
           rwkv_mu, rwkv_w0, rwkv_w2, rwkv_a0, rwkv_a2, rwkv_g2, rwkv_k_k, rwkv_k_a, rwkv_r_k, rwkv_ln_g,
           rwkv_ln_b, s5_a_re, s5_a_im, s5_b_re, s5_b_im, s5_c_re, s5_c_im, s5_d, s5_log_dt, s5_glu_w,
           s5_glu_b, w_branch, w_out, norm_ffn_g, ffn_w1, ffn_w3, ffn_w2, norm_final_g):
    b, t, _ = x.shape
    x2 = x.reshape(b * t, D_MODEL).astype(F32)
    seq = lambda a: a.reshape(b, t, a.shape[-1])
    bias_tab = _rel_bias_table(rel_bias)
    for i in range(DEPTH):
        z, xbc, q, k, v, qi, small, kidx, rwkv_cols, s5_u, gates = _proj(x2, norm_mix_g[i], _pack_w_in(w_in[i]))
        small = seq(small)
        y_a = _ssd(seq(z), seq(xbc), small, ssd_conv_w[i], ssd_conv_b[i], ssd_dt_bias[i], ssd_a_log[i],
                   ssd_d[i], ssd_norm_g[i])
        y_b = _dsa(seq(q), seq(k), seq(v), seq(qi), seq(kidx), small, bias_tab)
        y_c, g_c = _rwkv(seq(rwkv_cols), rwkv_mu[i], rwkv_w0[i], rwkv_w2[i], rwkv_a0[i], rwkv_a2[i],
                         rwkv_g2[i], rwkv_k_k[i], rwkv_k_a[i], rwkv_r_k[i], rwkv_ln_g[i], rwkv_ln_b[i])
        tables = _s5_tables(s5_a_re[i], s5_a_im[i], s5_b_re[i], s5_b_im[i], s5_c_re[i], s5_c_im[i],
                            s5_log_dt[i])
        y_d = _s5(seq(s5_u), tables, s5_d[i], s5_glu_w[i], s5_glu_b[i])
        x2 = _merge(x2, y_a.reshape(b * t, BRANCH_WIDTH), y_b.reshape(b * t, BRANCH_WIDTH), y_c, g_c, y_d,
                    gates, w_branch[i].astype(BF16), w_out[i].astype(BF16))
        x2 = _ffn(x2, norm_ffn_g[i], ffn_w1[i].astype(BF16), ffn_w3[i].astype(BF16), ffn_w2[i].astype(BF16),
                  norm_final_g, final_norm=(i == DEPTH - 1))
    return x2.reshape(b, t, D_MODEL)
```

```python
import functools
import math

import jax
import jax.numpy as jnp
import numpy as np
from jax import lax
from jax.experimental import pallas as pl
from jax.experimental.pallas import tpu as pltpu

D_MODEL = 1024
DEPTH = 4
CHUNK = 64
N_BRANCH = 4
BRANCH_WIDTH = 512
HEAD_DIM = 64
N_HEADS = 8
RMS_EPS = 1e-6

SSD_GROUPS = 2
SSD_STATE = 64
SSD_CONV = 4
SSD_XBC = 768

IDX_HEADS = 4
IDX_DIM = 32
DSA_MAX_TOPK = 256
REL_BUCKETS = 32
REL_MAX_DIST = 256

RWKV_COLS = 1792
RWKV_GN_EPS = 64e-5

S5_GROUP = 16
S5_GROUPS = 32
S5_STATE = 64

FFN_HIDDEN = 2816

LANES = 128
SUBLANES = 8
VMEM_LIMIT = 56 * 1024 * 1024

F32 = jnp.float32
BF16 = jnp.bfloat16
HIGHEST = lax.Precision.HIGHEST


def _params(*sem):
    return pltpu.CompilerParams(dimension_semantics=sem, vmem_limit_bytes=VMEM_LIMIT)


def _resident(shape):
    nd = len(shape)
    return pl.BlockSpec(shape, lambda *_: (0,) * nd, pipeline_mode=pl.Buffered(1))


def _dot(a, b):
    return jnp.dot(a, b, preferred_element_type=F32)


def _dot_exact(a, b):
    return jnp.dot(a, b, preferred_element_type=F32, precision=HIGHEST)


def _sigmoid(x):
    return 1.0 / (1.0 + jnp.exp(-x))


def _silu(x):
    return x * _sigmoid(x)


def _softplus(x):
    return jnp.maximum(x, 0.0) + jnp.log(1.0 + jnp.exp(-jnp.abs(x)))


PROJ_SEGS = (
    ("z", 512, F32), ("xbc", 768, F32), ("q", 512, BF16), ("k", 512, BF16), ("v", 512, BF16),
    ("qi", 128, F32), ("small", 128, F32), ("kidx", 128, BF16), ("rwkv", RWKV_COLS, F32), ("s5", 512, F32),
    ("gates", 4096, F32),
)
PROJ_COLS = sum(s[1] for s in PROJ_SEGS)


def _pack_w_in(w):
    o = np.cumsum((0, 512, 768, 8, 512, 512, 512, 128, 32, 4, RWKV_COLS, 512, 4096))
    seg = lambda i: w[:, o[i]:o[i + 1]]
    small = jnp.concatenate([seg(2), seg(7), seg(8), jnp.zeros((D_MODEL, LANES - 44), w.dtype)], axis=1)
    kidx = jnp.tile(seg(7), (1, IDX_HEADS))
    parts = [seg(0), seg(1), seg(3), seg(4), seg(5), seg(6), small, kidx, seg(9), seg(10), seg(11)]
    return jnp.concatenate(parts, axis=1).astype(BF16)


def _proj_kernel(x_ref, g_ref, w_ref, *out_refs):
    x = x_ref[...]
    ms = jnp.mean(x * x, axis=-1, keepdims=True)
    h = (x * lax.rsqrt(ms + RMS_EPS) * g_ref[...]).astype(BF16)
    off = 0
    for o_ref, (_, n, _) in zip(out_refs, PROJ_SEGS):
        o_ref[...] = _dot(h, w_ref[:, off:off + n]).astype(o_ref.dtype)
        off += n


def _proj(x2, g, w_packed, tm=256):
    m = x2.shape[0]
    tm = min(tm, m)
    return pl.pallas_call(
        _proj_kernel,
        grid=(m // tm,),
        in_specs=[pl.BlockSpec((tm, D_MODEL), lambda i: (i, 0)),
                  _resident((1, D_MODEL)), _resident((D_MODEL, PROJ_COLS))],
        out_specs=[pl.BlockSpec((tm, n), lambda i: (i, 0)) for _, n, _ in PROJ_SEGS],
        out_shape=[jax.ShapeDtypeStruct((m, n), dt) for _, n, dt in PROJ_SEGS],
        compiler_params=_params("parallel"),
        name="proj",
    )(x2, g.reshape(1, D_MODEL), w_packed)


def _merge_kernel(x_ref, ya_ref, yb_ref, yc_ref, gc_ref, yd_ref, gate_ref, wb_ref, wo_ref, o_ref):
    ys = (ya_ref[...], yb_ref[...], (yc_ref[...] * gc_ref[...]).astype(BF16), yd_ref[...])
    merged = None
    for n, y in enumerate(ys):
        gate = _sigmoid(gate_ref[:, n * D_MODEL:(n + 1) * D_MODEL])
        term = gate * _dot(y, wb_ref[n])
        merged = term if merged is None else merged + term
    o_ref[...] = x_ref[...] + _dot(merged.astype(BF16), wo_ref[...])


def _merge(x2, y_a, y_b, y_c, g_c, y_d, gates, wb, wo, tm=256):
    m = x2.shape[0]
    tm = min(tm, m)
    row = lambda n: pl.BlockSpec((tm, n), lambda i: (i, 0))
    return pl.pallas_call(
        _merge_kernel,
        grid=(m // tm,),
        in_specs=[row(D_MODEL)] + [row(BRANCH_WIDTH)] * 5 + [row(N_BRANCH * D_MODEL),
                  _resident((N_BRANCH, BRANCH_WIDTH, D_MODEL)), _resident((D_MODEL, D_MODEL))],
        out_specs=row(D_MODEL),
        out_shape=jax.ShapeDtypeStruct((m, D_MODEL), F32),
        compiler_params=_params("parallel"),
        name="merge",
    )(x2, y_a, y_b, y_c, g_c, y_d, gates, wb, wo)


def _ffn_kernel(x_ref, g_ref, w1_ref, w3_ref, w2_ref, gf_ref, o_ref, *, final_norm):
    x = x_ref[...]
    ms = jnp.mean(x * x, axis=-1, keepdims=True)
    h = (x * lax.rsqrt(ms + RMS_EPS) * g_ref[...]).astype(BF16)
    hid = _silu(_dot(h, w1_ref[...])) * _dot(h, w3_ref[...])
    x = x + _dot(hid.astype(BF16), w2_ref[...])
    if final_norm:
        ms = jnp.mean(x * x, axis=-1, keepdims=True)
        x = x * lax.rsqrt(ms + RMS_EPS) * gf_ref[...]
    o_ref[...] = x


def _ffn(x2, g, w1, w3, w2, g_final, final_norm, tm=256):
    m = x2.shape[0]
    tm = min(tm, m)
    row = pl.BlockSpec((tm, D_MODEL), lambda i: (i, 0))
    return pl.pallas_call(
        functools.partial(_ffn_kernel, final_norm=final_norm),
        grid=(m // tm,),
        in_specs=[row, _resident((1, D_MODEL)), _resident((D_MODEL, FFN_HIDDEN)),
                  _resident((D_MODEL, FFN_HIDDEN)), _resident((FFN_HIDDEN, D_MODEL)),
                  _resident((1, D_MODEL))],
        out_specs=row,
        out_shape=jax.ShapeDtypeStruct((m, D_MODEL), F32),
        compiler_params=_params("parallel"),
        name="ffn",
    )(x2, g.reshape(1, D_MODEL), w1, w3, w2, g_final.reshape(1, D_MODEL))


def _head_expand():
    e = np.zeros((LANES, BRANCH_WIDTH), np.float32)
    for h in range(N_HEADS):
        e[h, h * HEAD_DIM:(h + 1) * HEAD_DIM] = 1.0
    return jnp.asarray(e)


def _head_select():
    s = np.zeros((N_HEADS, BRANCH_WIDTH), np.float32)
    for h in range(N_HEADS):
        s[h, h * HEAD_DIM] = 1.0
    return jnp.asarray(s)


def _ssd_kernel(z_ref, xbc_ref, xprev_ref, small_ref, cw_ref, cb_ref, dtb_ref, alog_ref, dsk_ref,
                ng_ref, exp_ref, sel_ref, o_ref, xpad_scr, state_scr, y_scr, *, L):
    c = pl.program_id(1)

    @pl.when(c == 0)
    def _():
        state_scr[...] = jnp.zeros_like(state_scr)

    xpad_scr[0:SUBLANES, :] = jnp.where(c == 0, 0.0, xprev_ref[0])
    xpad_scr[SUBLANES:SUBLANES + L, :] = xbc_ref[0]
    acc = cb_ref[...]
    for j in range(SSD_CONV):
        lo = SUBLANES - (SSD_CONV - 1) + j
        acc = acc + cw_ref[j:j + 1, :] * xpad_scr[lo:lo + L, :]
    xc = _silu(acc)
    xs = xc[:, :BRANCH_WIDTH]

    dt = _softplus(_dot_exact(small_ref[0], exp_ref[...]) + dtb_ref[...])
    a = dt * (-jnp.exp(alog_ref[...]))
    rows = lax.broadcasted_iota(jnp.int32, (L, L), 0)
    cols = lax.broadcasted_iota(jnp.int32, (L, L), 1)
    causal = rows >= cols
    acs = _dot_exact(causal.astype(F32), a)
    acs_t = lax.dot_general(sel_ref[...], acs, (((1,), (1,)), ((), ())),
                            precision=HIGHEST, preferred_element_type=F32)
    xdt = xs * dt

    for g in range(SSD_GROUPS):
        b_g = xc[:, BRANCH_WIDTH + g * SSD_STATE:BRANCH_WIDTH + (g + 1) * SSD_STATE].astype(BF16)
        c_g = xc[:, BRANCH_WIDTH + (SSD_GROUPS + g) * SSD_STATE:
                 BRANCH_WIDTH + (SSD_GROUPS + g + 1) * SSD_STATE].astype(BF16)
        cb = lax.dot_general(c_g, b_g, (((1,), (1,)), ((), ())), preferred_element_type=F32)
        for hh in range(N_HEADS // SSD_GROUPS):
            h = g * (N_HEADS // SSD_GROUPS) + hh
            sl = slice(h * HEAD_DIM, (h + 1) * HEAD_DIM)
            col = acs[:, h * HEAD_DIM:h * HEAD_DIM + 1]
            row = acs_t[h:h + 1, :]
            dec = jnp.where(causal, jnp.exp(jnp.minimum(col - row, 0.0)), 0.0)
            x_h = xdt[:, sl]
            st = state_scr[h]
            y_h = _dot((cb * dec).astype(BF16), x_h.astype(BF16))
            y_h = y_h + _dot(c_g, st.astype(BF16)) * jnp.exp(col)
            y_scr[:, sl] = y_h
            last = acs[L - 1:L, h * HEAD_DIM:h * HEAD_DIM + 1]
            x_w = (x_h * jnp.exp(last - col)).astype(BF16)
            state_scr[h] = st * jnp.exp(last) + lax.dot_general(
                b_g, x_w, (((0,), (0,)), ((), ())), preferred_element_type=F32)

    y = y_scr[...] + dsk_ref[...] * xs
    y = y * _silu(z_ref[0])
    half = BRANCH_WIDTH // SSD_GROUPS
    for g in range(SSD_GROUPS):
        yg = y[:, g * half:(g + 1) * half]
        yg = yg * lax.rsqrt(jnp.mean(yg * yg, axis=-1, keepdims=True) + RMS_EPS)
        o_ref[0, :, g * half:(g + 1) * half] = (yg * ng_ref[:, g * half:(g + 1) * half]).astype(o_ref.dtype)


def _ssd(z, xbc, small, conv_w, conv_b, dt_bias, a_log, d_skip, norm_g, L=256):
    b, t, _ = z.shape
    rep = lambda p: jnp.repeat(p.astype(F32), HEAD_DIM).reshape(1, BRANCH_WIDTH)
    blk = lambda n: pl.BlockSpec((1, L, n), lambda i, j: (i, j, 0))
    prev = pl.BlockSpec((1, SUBLANES, SSD_XBC),
                        lambda i, j: (i, jnp.maximum(j * (L // SUBLANES) - 1, 0), 0))
    return pl.pallas_call(
        functools.partial(_ssd_kernel, L=L),
        grid=(b, t // L),
        in_specs=[blk(BRANCH_WIDTH), blk(SSD_XBC), prev, blk(LANES),
                  _resident((SSD_CONV, SSD_XBC)), _resident((1, SSD_XBC)),
                  _resident((1, BRANCH_WIDTH)), _resident((1, BRANCH_WIDTH)), _resident((1, BRANCH_WIDTH)),
                  _resident((1, BRANCH_WIDTH)), _resident((LANES, BRANCH_WIDTH)),
                  _resident((N_HEADS, BRANCH_WIDTH))],
        out_specs=blk(BRANCH_WIDTH),
        out_shape=jax.ShapeDtypeStruct((b, t, BRANCH_WIDTH), BF16),
        scratch_shapes=[pltpu.VMEM((SUBLANES + L, SSD_XBC), F32),
                        pltpu.VMEM((N_HEADS, SSD_STATE, HEAD_DIM), F32),
                        pltpu.VMEM((L, BRANCH_WIDTH), F32)],
        compiler_params=_params("parallel", "arbitrary"),
        name="ssd",
    )(z, xbc, xbc, small, conv_w, conv_b.reshape(1, SSD_XBC), rep(dt_bias), rep(a_log), rep(d_skip),
      norm_g.reshape(1, BRANCH_WIDTH), _head_expand(), _head_select())


S5_CHUNK = 64
S5_ROW = S5_CHUNK * S5_GROUP


def _s5_tables(a_re, a_im, b_re, b_im, c_re, c_im, log_dt):
    L = S5_CHUNK
    f = lambda v: v.astype(F32)
    a_re, a_im, b_re, b_im, c_re, c_im = map(f, (a_re, a_im, b_re, b_im, c_re, c_im))
    dt = jnp.exp(f(log_dt))[:, None]
    lr, li = a_re * dt, a_im * dt
    tau = jnp.arange(L + 1, dtype=F32)[None, :, None]
    mag = jnp.exp(lr[:, None, :] * tau)
    pr, pi = mag * jnp.cos(li[:, None, :] * tau), mag * jnp.sin(li[:, None, :] * tau)
    abr, abi = pr[:, 1], pi[:, 1]
    den = a_re * a_re + a_im * a_im
    qr = ((abr - 1.0) * a_re + abi * a_im) / den
    qi = (abi * a_re - (abr - 1.0) * a_im) / den
    bbr = qr[..., None] * b_re - qi[..., None] * b_im
    bbi = qr[..., None] * b_im + qi[..., None] * b_re
    pbr = pr[..., None] * bbr[:, None] - pi[..., None] * bbi[:, None]
    pbi = pr[..., None] * bbi[:, None] + pi[..., None] * bbr[:, None]
    kern = (jnp.einsum("gop,gtpi->gtoi", c_re, pbr[:, :L], precision=HIGHEST)
            - jnp.einsum("gop,gtpi->gtoi", c_im, pbi[:, :L], precision=HIGHEST))
    tz = _s5_toeplitz(kern.transpose(0, 3, 1, 2).reshape(S5_GROUPS, S5_GROUP, S5_ROW))
    rev = L - 1 - jnp.arange(L)
    ws = jnp.concatenate([pbr[:, rev], pbi[:, rev]], axis=2)
    ws = ws.transpose(0, 1, 3, 2).reshape(S5_GROUPS, S5_ROW, 2 * S5_STATE)
    cpr = c_re[:, None] * pr[:, 1:, None, :] - c_im[:, None] * pi[:, 1:, None, :]
    cpi = c_re[:, None] * pi[:, 1:, None, :] + c_im[:, None] * pr[:, 1:, None, :]
    wy = jnp.concatenate([cpr, -cpi], axis=3).transpose(0, 3, 1, 2).reshape(S5_GROUPS, 2 * S5_STATE, S5_ROW)
    a1 = jnp.concatenate([pr[:, L], pr[:, L]], axis=1)[:, None, :]
    a2 = jnp.concatenate([-pi[:, L], pi[:, L]], axis=1)[:, None, :]
    return tz, ws.astype(BF16), wy.astype(BF16), a1, a2


def _s5_toeplitz_kernel(k_ref, o_ref):
    kf = k_ref[0]
    col = lax.broadcasted_iota(jnp.int32, kf.shape, 1)
    for s in range(S5_CHUNK):
        shifted = pltpu.roll(kf, s * S5_GROUP, axis=1) if s else kf
        o_ref[0, s * S5_GROUP:(s + 1) * S5_GROUP, :] = jnp.where(col >= s * S5_GROUP, shifted, 0.0).astype(BF16)


def _s5_toeplitz(kflat):
    return pl.pallas_call(
        _s5_toeplitz_kernel,
        grid=(S5_GROUPS,),
        in_specs=[pl.BlockSpec((1, S5_GROUP, S5_ROW), lambda g: (g, 0, 0))],
        out_specs=pl.BlockSpec((1, S5_ROW, S5_ROW), lambda g: (g, 0, 0)),
        out_shape=jax.ShapeDtypeStruct((S5_GROUPS, S5_ROW, S5_ROW), BF16),
        compiler_params=_params("parallel"),
        name="s5_toeplitz",
    )(kflat)


def _s5_kernel(u_ref, tz_ref, ws_ref, wy_ref, a1_ref, a2_ref, y_ref, sl_scr, x_scr, *, nb, nc):
    u = u_ref[0]
    y_local = _dot(u, tz_ref[0])
    sl_scr[...] = _dot(u, ws_ref[0])
    a1, a2 = a1_ref[0], a2_ref[0]

    def step(c, x):
        r = pl.ds(pl.multiple_of(c * nb, SUBLANES), nb)
        x_scr[r, :] = x
        return a1 * x + a2 * pltpu.roll(x, S5_STATE, axis=1) + sl_scr[r, :]

    lax.fori_loop(0, nc, step, jnp.zeros((nb, 2 * S5_STATE), F32))
    y_ref[0] = y_local + _dot(x_scr[...].astype(BF16), wy_ref[0])


def _s5_out_kernel(y_ref, u_ref, d_ref, w_ref, b_ref, o_ref):
    y = y_ref[...] + d_ref[...] * u_ref[...]
    y = 0.5 * y * (1.0 + jnp.tanh(math.sqrt(2.0 / math.pi) * (y + 0.044715 * (y * y * y))))
    o_ref[...] = (y * _sigmoid(_dot(y.astype(BF16), w_ref[...]) + b_ref[...])).astype(o_ref.dtype)


def _s5(u, tables, d_skip, glu_w, glu_b, tm=512):
    b, t, _ = u.shape
    nc = t // S5_CHUNK
    m = nc * b
    tz, ws, wy, a1, a2 = tables
    ug = u.reshape(b, nc, S5_CHUNK, S5_GROUPS, S5_GROUP).transpose(3, 1, 0, 2, 4)
    ug = ug.reshape(S5_GROUPS, m, S5_ROW).astype(BF16)
    grp = lambda r, c: pl.BlockSpec((1, r, c), lambda g: (g, 0, 0))
    y = pl.pallas_call(
        functools.partial(_s5_kernel, nb=b, nc=nc),
        grid=(S5_GROUPS,),
        in_specs=[grp(m, S5_ROW), grp(S5_ROW, S5_ROW), grp(S5_ROW, 2 * S5_STATE),
                  grp(2 * S5_STATE, S5_ROW), grp(1, 2 * S5_STATE), grp(1, 2 * S5_STATE)],
        out_specs=grp(m, S5_ROW),
        out_shape=jax.ShapeDtypeStruct((S5_GROUPS, m, S5_ROW), F32),
        scratch_shapes=[pltpu.VMEM((m, 2 * S5_STATE), F32), pltpu.VMEM((m, 2 * S5_STATE), F32)],
        compiler_params=_params("parallel"),
        name="s5_scan",
    )(ug, tz, ws, wy, a1, a2)
    y = y.reshape(S5_GROUPS, nc, b, S5_CHUNK, S5_GROUP).transpose(2, 1, 3, 0, 4).reshape(b * t, BRANCH_WIDTH)
    tm = min(tm, b * t)
    row = pl.BlockSpec((tm, BRANCH_WIDTH), lambda i: (i, 0))
    return pl.pallas_call(
        _s5_out_kernel,
        grid=(b * t // tm,),
        in_specs=[row, row, _resident((1, BRANCH_WIDTH)), _resident((BRANCH_WIDTH, BRANCH_WIDTH)),
                  _resident((1, BRANCH_WIDTH))],
        out_specs=row,
        out_shape=jax.ShapeDtypeStruct((b * t, BRANCH_WIDTH), BF16),
        compiler_params=_params("parallel"),
        name="s5_out",
    )(y, u.reshape(b * t, BRANCH_WIDTH), d_skip.reshape(1, BRANCH_WIDTH).astype(F32), glu_w.astype(BF16),
      glu_b.reshape(1, BRANCH_WIDTH).astype(F32))


RWKV_LORA_OFF = 3 * BRANCH_WIDTH
RWKV_GATE_OFF = RWKV_LORA_OFF + LANES


def _rwkv_pre_kernel(p_ref, prev_ref, mu_ref, w0_ref, w2_ref, a0_ref, a2_ref, g2_ref, kk_ref, ka_ref,
                     r_o, w_o, k_o, v_o, kk_o, a_o, g_o, pad_scr, *, tm):
    j = pl.program_id(1)
    pad_scr[0:SUBLANES, :] = jnp.where(j == 0, 0.0, prev_ref[0])
    pad_scr[SUBLANES:SUBLANES + tm, :] = p_ref[0]
    p = p_ref[0]
    p = p + (pad_scr[SUBLANES - 1:SUBLANES - 1 + tm, :] - p) * mu_ref[...]
    r = p[:, 0:BRANCH_WIDTH]
    k = p[:, BRANCH_WIDTH:2 * BRANCH_WIDTH]
    v = p[:, 2 * BRANCH_WIDTH:3 * BRANCH_WIDTH]
    lora = p[:, RWKV_LORA_OFF:RWKV_GATE_OFF]
    dg = p[:, RWKV_GATE_OFF:RWKV_COLS]
    w = -_softplus(-(w0_ref[...] + _dot(jnp.tanh(lora).astype(BF16), w2_ref[...]))) - 0.5
    a = _sigmoid(a0_ref[...] + _dot(lora.astype(BF16), a2_ref[...]))
    r_o[0] = r
    w_o[0] = jnp.exp(-jnp.exp(w))
    k_o[0] = k * (1.0 + (a - 1.0) * ka_ref[...])
    v_o[0] = v
    kk_o[0] = k * kk_ref[...]
    a_o[0] = a
    g_o[0] = _dot(_sigmoid(dg).astype(BF16), g2_ref[...])


def _rwkv_scan_kernel(r_ref, w_ref, k_ref, v_ref, kk_ref, a_ref, rk_ref, lng_ref, lnb_ref, y_ref,
                      s_scr, even_scr, odd_scr, *, tb):
    nl = s_scr.shape[2]

    @pl.when(pl.program_id(0) == 0)
    def _():
        s_scr[...] = jnp.zeros_like(s_scr)

    def stage(t, buf):
        tile = lambda ref: ref[:, t].reshape(nl, HEAD_DIM).T
        kk = tile(kk_ref)
        kk = kk / jnp.maximum(jnp.sqrt(jnp.sum(kk * kk, axis=0, keepdims=True)), 1e-12)
        for i, x in enumerate((tile(r_ref), tile(w_ref), tile(k_ref), tile(v_ref), -kk, kk * tile(a_ref))):
            buf[i] = x

    def step(t, buf):
        row = lambda i, kx: buf[i, pl.ds(kx, 1), :]

        def s_times_a(kb, sa):
            for j in range(SUBLANES):
                kx = kb * SUBLANES + j
                sa = sa + s_scr[kx] * row(4, kx)
            return sa

        sa = jnp.zeros(s_scr.shape[1:], F32)
        for kb in range(HEAD_DIM // SUBLANES):
            sa = s_times_a(kb, sa)
        v_t = buf[3]

        def update(kb, y):
            for j in range(SUBLANES):
                kx = kb * SUBLANES + j
                s_new = s_scr[kx] * row(1, kx) + sa * row(5, kx) + v_t * row(2, kx)
                s_scr[kx] = s_new
                y = y + s_new * row(0, kx)
            return y

        y = lax.fori_loop(0, HEAD_DIM // SUBLANES, update, jnp.zeros(s_scr.shape[1:], F32))
        mean = jnp.mean(y, axis=0, keepdims=True)
        var = jnp.mean(jnp.square(y - mean), axis=0, keepdims=True)
        y = (y - mean) * lax.rsqrt(var + RWKV_GN_EPS) * lng_ref[0] + lnb_ref[0]
        y = y + jnp.sum(buf[0] * buf[2] * rk_ref[0], axis=0, keepdims=True) * v_t
        y_ref[:, t] = y.T.reshape(nl // N_HEADS, N_HEADS, HEAD_DIM)

    def two_steps(i, carry):
        t = 2 * i
        stage(t + 1, odd_scr)
        step(t, even_scr)
        stage(jnp.minimum(t + 2, tb - 1), even_scr)
        step(t + 1, odd_scr)
        return carry

    stage(0, even_scr)
    lax.fori_loop(0, tb // 2, two_steps, 0)


def _rwkv(cols, mu, w0, w2, a0, a2, g2, k_k, k_a, r_k, ln_g, ln_b, tm=256, tb=32):
    b, t, _ = cols.shape
    tm, tb = min(tm, t), min(tb, t)
    vec = lambda p: p.reshape(1, -1).astype(F32)
    zpad = jnp.zeros((LANES // 2, BRANCH_WIDTH), F32)
    blk = lambda n: pl.BlockSpec((1, tm, n), lambda i, j: (i, j, 0))
    prev = pl.BlockSpec((1, SUBLANES, RWKV_COLS),
                        lambda i, j: (i, jnp.maximum(j * (tm // SUBLANES) - 1, 0), 0))
    outs = pl.pallas_call(
        functools.partial(_rwkv_pre_kernel, tm=tm),
        grid=(b, t // tm),
        in_specs=[blk(RWKV_COLS), prev, _resident((1, RWKV_COLS)), _resident((1, BRANCH_WIDTH)),
                  _resident((LANES, BRANCH_WIDTH)), _resident((1, BRANCH_WIDTH)),
                  _resident((LANES, BRANCH_WIDTH)), _resident((LANES, BRANCH_WIDTH)),
                  _resident((1, BRANCH_WIDTH)), _resident((1, BRANCH_WIDTH))],
        out_specs=[blk(BRANCH_WIDTH)] * 7,
        out_shape=[jax.ShapeDtypeStruct((b, t, BRANCH_WIDTH), F32)] * 7,
        scratch_shapes=[pltpu.VMEM((SUBLANES + tm, RWKV_COLS), F32)],
        compiler_params=_params("parallel", "arbitrary"),
        name="rwkv_pre",
    )(cols, cols, vec(mu), vec(w0), jnp.concatenate([w2.astype(F32), zpad]).astype(BF16), vec(a0),
      jnp.concatenate([zpad, a2.astype(F32)]).astype(BF16), g2.astype(BF16), vec(k_k), vec(k_a))
    *seqs, gate = outs
    per_head = lambda p: jnp.tile(p.astype(F32).reshape(N_HEADS, HEAD_DIM).T, (1, b))[None]
    nl = b * N_HEADS
    heads = lambda s: s.reshape(b, t, N_HEADS, HEAD_DIM)
    tblk = pl.BlockSpec((b, tb, N_HEADS, HEAD_DIM), lambda i: (0, i, 0, 0))
    tiles = pltpu.VMEM((6, HEAD_DIM, nl), F32)
    y = pl.pallas_call(
        functools.partial(_rwkv_scan_kernel, tb=tb),
        grid=(t // tb,),
        in_specs=[tblk] * 6 + [_resident((1, HEAD_DIM, nl))] * 3,
        out_specs=tblk,
        out_shape=jax.ShapeDtypeStruct((b, t, N_HEADS, HEAD_DIM), F32),
        scratch_shapes=[pltpu.VMEM((HEAD_DIM, HEAD_DIM, nl), F32), tiles, tiles],
        compiler_params=_params("arbitrary"),
        name="rwkv_scan",
    )(*[heads(s) for s in seqs], per_head(r_k), per_head(ln_g), per_head(ln_b))
    return y.reshape(b * t, BRANCH_WIDTH), gate.reshape(b * t, BRANCH_WIDTH)


DSA_TQ = 128
DSA_TK = 512
INT_MIN = -2 ** 31
MASKED = -1e30


def _bias_tiles():
    n = np.arange(1, 8192, dtype=np.float32)
    large = 8 + (np.log(n / np.float32(8)) / np.float32(math.log(REL_MAX_DIST / 8)) * np.float32(8)).astype(np.int32)
    n_sat = int(n[np.argmax(large >= 15)])
    return -(-(n_sat + DSA_TQ - 1) // DSA_TQ) + 1


def _bias_kernel(rb_ref, o_ref):
    e = pl.program_id(0)
    t = DSA_TQ
    rel = (lax.broadcasted_iota(jnp.int32, (t, t), 0) - lax.broadcasted_iota(jnp.int32, (t, t), 1)) - e * t
    half, exact = REL_BUCKETS // 2, REL_BUCKETS // 4
    n = jnp.abs(rel)
    n_f = jnp.maximum(n, 1).astype(F32)
    large = exact + (jnp.log(n_f / exact) / math.log(REL_MAX_DIST / exact) * (half - exact)).astype(jnp.int32)
    large = jnp.minimum(large, half - 1)
    bucket = jnp.where(rel > 0, half, 0) + jnp.where(n < exact, n, large)
    for h in range(N_HEADS):
        acc = jnp.zeros((t, t), F32)
        for bkt in range(REL_BUCKETS):
            acc = jnp.where(bucket == bkt, rb_ref[bkt, h], acc)
        o_ref[h // 2, 0, :, (h % 2) * t:(h % 2 + 1) * t] = acc


def _rel_bias_table(rel_bias):
    nd = _bias_tiles()
    t = DSA_TQ
    return pl.pallas_call(
        _bias_kernel,
        grid=(nd,),
        in_specs=[pl.BlockSpec(memory_space=pltpu.SMEM)],
        out_specs=pl.BlockSpec((N_HEADS // 2, 1, t, 2 * t), lambda e: (0, e, 0, 0)),
        out_shape=jax.ShapeDtypeStruct((N_HEADS // 2, nd, t, 2 * t), F32),
        compiler_params=_params("parallel"),
        name="rel_bias",
    )(rel_bias.astype(F32))


def _nt_dot(a, b, **kw):
    return lax.dot_general(a, b, (((1,), (1,)), ((), ())), preferred_element_type=F32, **kw)


def _dsa_kernel(q_ref, k_ref, vt_ref, qi_ref, kidx_ref, small_ref, bias_ref, tri_ref, pick_ref, o_ref,
                key_scr, msk_scr, lga_scr, lgb_scr, sa_scr, sb_scr, *, topk, nd):
    tq, tk = DSA_TQ, DSA_TK
    th = tk // 2
    q0 = pl.program_id(1) * tq
    nkb = (q0 + tq + tk - 1) // tk
    int_min = jnp.int32(INT_MIN)
    q_pos = q0 + lax.broadcasted_iota(jnp.int32, (1, tq), 1)
    lim = (q_pos // CHUNK + 1) * CHUNK
    lane = lax.broadcasted_iota(jnp.int32, (tq, LANES), 1)

    qi = qi_ref[0]
    qi_all = jnp.concatenate(
        [jnp.where((lane >= h * IDX_DIM) & (lane < (h + 1) * IDX_DIM), qi, 0.0).astype(BF16)
         for h in range(IDX_HEADS)], axis=0)
    w_t = _nt_dot(pick_ref[...], small_ref[0], precision=HIGHEST)
    w_h = [w_t[h:h + 1, :] for h in range(IDX_HEADS)]

    def raw_scores(j, buf):
        s0 = pl.multiple_of(j * tk, tk)
        buf[...] = _nt_dot(kidx_ref[0, pl.ds(s0, tk), :], qi_all)

    def finish_scores(j, buf):
        s0 = pl.multiple_of(j * tk, tk)
        s = None
        for h in range(IDX_HEADS):
            term = w_h[h] * jnp.maximum(buf[:, h * tq:(h + 1) * tq], 0.0)
            s = term if s is None else s + term
        s = s + 0.0
        bits = lax.bitcast_convert_type(s, jnp.int32)
        bits = jnp.where(bits < 0, bits ^ jnp.int32(0x7FFFFFFF), bits)
        pos = s0 + lax.broadcasted_iota(jnp.int32, (tk, tq), 0)
        key_scr[pl.ds(s0, tk), :] = jnp.where(pos < lim, bits, int_min)

    def score_two(i, carry):
        j = 2 * i
        last = nkb - 1
        raw_scores(jnp.minimum(j + 1, last), sb_scr)
        finish_scores(j, sa_scr)
        raw_scores(jnp.minimum(j + 2, last), sa_scr)
        finish_scores(jnp.minimum(j + 1, last), sb_scr)
        return carry

    raw_scores(0, sa_scr)
    lax.fori_loop(0, (nkb + 1) // 2, score_two, 0)

    acc_rows = 4 * SUBLANES

    def count(pred):
        def body(j, acc):
            hit = jnp.where(pred(key_scr[pl.ds(pl.multiple_of(j * tk, tk), tk), :]), 1.0, 0.0)
            return acc + jnp.sum(hit.reshape(tk // acc_rows, acc_rows, tq), axis=0)
        acc = lax.fori_loop(0, nkb, body, jnp.zeros((acc_rows, tq), F32))
        return jnp.sum(acc, axis=0, keepdims=True)

    kf = jnp.float32(topk)
    c0 = count(lambda kb: kb >= 0)
    thr = jnp.where(c0 >= kf, jnp.int32(0), int_min)
    n_ge = jnp.where(c0 >= kf, c0, 0.0)

    def search(it, carry):
        thr, n_ge = carry
        cand = thr + jnp.left_shift(jnp.int32(1), 30 - it)
        c = count(lambda kb: kb >= cand)
        return jnp.where(c >= kf, cand, thr), jnp.where(c >= kf, c, n_ge)

    thr, n_ge = lax.fori_loop(0, 31, search, (thr, n_ge))
    real = thr > int_min
    surplus = jnp.max(jnp.where(real, n_ge - kf, 0.0))

    def select_plain(j, carry):
        s0 = pl.multiple_of(j * tk, tk)
        kb = key_scr[pl.ds(s0, tk), :]
        msk_scr[pl.ds(s0, tk), :] = jnp.where((kb >= thr) & (kb > int_min), 0.0, MASKED)
        return carry

    @pl.when(surplus <= 0.0)
    def _():
        lax.fori_loop(0, nkb, select_plain, 0)

    @pl.when(surplus > 0.0)
    def _():
        quota = kf - count(lambda kb: kb > thr)

        def select_ranked(j, run):
            s0 = pl.multiple_of(j * tk, tk)
            kb = key_scr[pl.ds(s0, tk), :]
            eq = kb == thr
            rank = run + _dot(tri_ref[...], jnp.where(eq, 1.0, 0.0).astype(BF16))
            sel = (kb > thr) | (eq & real & (rank <= quota))
            msk_scr[pl.ds(s0, tk), :] = jnp.where(sel, 0.0, MASKED)
            return rank[tk - 1:tk, :]

        lax.fori_loop(0, nkb, select_ranked, jnp.zeros((1, tq), F32))

    q = q_ref[0]
    dim = lax.broadcasted_iota(jnp.int32, (LANES, tq), 0)
    for hp in range(N_HEADS // 2):
        q_pair = q[:, hp * LANES:(hp + 1) * LANES] * HEAD_DIM ** -0.5
        q2 = jnp.concatenate([jnp.where(lane < HEAD_DIM, q_pair, 0.0).astype(BF16),
                              jnp.where(lane >= HEAD_DIM, q_pair, 0.0).astype(BF16)], axis=0)

        def raw_logits(jh, hp=hp, q2=q2):
            s0 = pl.multiple_of(jh * th, th)
            return _nt_dot(k_ref[0, pl.ds(s0, th), hp * LANES:(hp + 1) * LANES], q2)

        def finish_logits(lg, jh, buf, hp=hp):
            s0 = pl.multiple_of(jh * th, th)
            e0 = (q0 - s0) // tq
            top = None
            for c in range(th // LANES):
                mask = msk_scr[pl.ds(s0 + c * LANES, LANES), :]
                lg_c = (lg[c * LANES:(c + 1) * LANES] + bias_ref[hp, jnp.clip(e0 - c, 0, nd - 1)]
                        + jnp.concatenate([mask, mask], axis=1))
                buf[c * LANES:(c + 1) * LANES, :] = lg_c
                top_c = jnp.max(lg_c, axis=0, keepdims=True)
                top = top_c if top is None else jnp.maximum(top, top_c)
            return top

        def softmax_pv(state, buf, top, j, half, hp=hp):
            m, l, acc = state
            m_new = jnp.maximum(m, top)
            alpha = jnp.exp(m - m_new)
            p = jnp.exp(buf[...] - m_new)
            l = alpha * l + jnp.sum(p, axis=0, keepdims=True)
            vt_blk = vt_ref[0, j, hp * LANES:(hp + 1) * LANES, half * th:(half + 1) * th]
            return m_new, l, alpha * acc + _dot(vt_blk, p.astype(BF16))

        def attend(j, carry):
            *state, top_a, top_b = carry
            ja = jnp.minimum(2 * j + 2, 2 * nkb - 2)
            lg_a, lg_b = raw_logits(ja), raw_logits(ja + 1)
            state = softmax_pv(state, lga_scr, top_a, j, 0)
            state = softmax_pv(state, lgb_scr, top_b, j, 1)
            return (*state, finish_logits(lg_a, ja, lga_scr), finish_logits(lg_b, ja + 1, lgb_scr))

        init = (jnp.full((1, 2 * tq), MASKED, F32), jnp.zeros((1, 2 * tq), F32), jnp.zeros((LANES, 2 * tq), F32),
                finish_logits(raw_logits(0), 0, lga_scr), finish_logits(raw_logits(1), 1, lgb_scr))
        _, l, acc, _, _ = lax.fori_loop(0, nkb, attend, init)
        o2 = acc / l
        out_t = jnp.where(dim < HEAD_DIM, o2[:, :tq], o2[:, tq:])
        o_ref[0, :, hp * LANES:(hp + 1) * LANES] = out_t.T.astype(o_ref.dtype)


def _dsa(q, k, v, qi, kidx, small, bias_tab):
    b, t, _ = q.shape
    topk = min(DSA_MAX_TOPK, t // 4)
    nd = bias_tab.shape[1]
    nblk = t // DSA_TK
    vt = v.reshape(b, nblk, DSA_TK, BRANCH_WIDTH).transpose(0, 1, 3, 2)
    tri = jnp.asarray(np.tril(np.ones((DSA_TK, DSA_TK), np.float32))).astype(BF16)
    pick = np.zeros((SUBLANES, LANES), np.float32)
    pick[np.arange(IDX_HEADS), 40 + np.arange(IDX_HEADS)] = 1.0
    qblk = lambda n: pl.BlockSpec((1, DSA_TQ, n), lambda i, j: (i, j, 0))
    full = lambda *s: pl.BlockSpec((1,) + s, lambda i, j: (i,) + (0,) * len(s))
    return pl.pallas_call(
        functools.partial(_dsa_kernel, topk=topk, nd=nd),
        grid=(b, t // DSA_TQ),
        in_specs=[qblk(BRANCH_WIDTH), full(t, BRANCH_WIDTH), full(nblk, BRANCH_WIDTH, DSA_TK), qblk(LANES),
                  full(t, LANES), qblk(LANES), _resident(bias_tab.shape), _resident((DSA_TK, DSA_TK)),
                  _resident((SUBLANES, LANES))],
        out_specs=qblk(BRANCH_WIDTH),
        out_shape=jax.ShapeDtypeStruct((b, t, BRANCH_WIDTH), BF16),
        scratch_shapes=[pltpu.VMEM((t, DSA_TQ), jnp.int32), pltpu.VMEM((t, DSA_TQ), F32),
                        pltpu.VMEM((DSA_TK // 2, 2 * DSA_TQ), F32), pltpu.VMEM((DSA_TK // 2, 2 * DSA_TQ), F32),
                        pltpu.VMEM((DSA_TK, IDX_HEADS * DSA_TQ), F32), pltpu.VMEM((DSA_TK, IDX_HEADS * DSA_TQ), F32)],
        compiler_params=_params("parallel", "arbitrary"),
        name="dsa",
    )(q, k, vt, qi, kidx, small, bias_tab, tri, jnp.asarray(pick))


def kernel(x, rel_bias, norm_mix_g, w_in, ssd_conv_w, ssd_conv_b, ssd_dt_bias, ssd_a_log, ssd_d, ssd_norm_g,
           rwkv_mu, rwkv_w0, rwkv_w2, rwkv_a0, rwkv_a2, rwkv_g2, rwkv_k_k, rwkv_k_a, rwkv_r_k, rwkv_ln_g,
           rwkv_ln_b, s5_a_re, s5_a_im, s5_b_re, s5_b_im, s5_c_re, s5_c_im, s5_d, s5_log_dt, s5_glu_w,
           s5_glu_b, w_branch, w_out, norm_ffn_g, ffn_w1, ffn_w3, ffn_w2, norm_final_g):
    b, t, _ = x.shape
    x2 = x.reshape(b * t, D_MODEL).astype(F32)
    seq = lambda a: a.reshape(b, t, a.shape[-1])
    bias_tab = _rel_bias_table(rel_bias)
    for i in range(DEPTH):
        z, xbc, q, k, v, qi, small, kidx, rwkv_cols, s5_u, gates = _proj(x2, norm_mix_g[i], _pack_w_in(w_in[i]))
        small = seq(small)
        y_a = _ssd(seq(z), seq(xbc), small, ssd_conv_w[i], ssd_conv_b[i], ssd_dt_bias[i], ssd_a_log[i],
                   ssd_d[i], ssd_norm_g[i])
        y_b = _dsa(seq(q), seq(k), seq(v), seq(qi), seq(kidx), small, bias_tab)
        y_c, g_c = _rwkv(seq(rwkv_cols), rwkv_mu[i], rwkv_w0[i], rwkv_w2[i], rwkv_a0[i], rwkv_a2[i],
                         rwkv_g2[i], rwkv_k_k[i], rwkv_k_a[i], rwkv_r_k[i], rwkv_ln_g[i], rwkv_ln_b[i])
        tables = _s5_tables(s5_a_re[i], s5_a_im[i], s5_b_re[i], s5_b_im[i], s5_c_re[i], s5_c_im[i],
                            s5_log_dt[i])
        y_d = _s5(seq(s5_u), tables, s5_d[i], s5_glu_w[i], s5_glu_b[i])
        x2 = _merge(x2, y_a.reshape(b * t, BRANCH_WIDTH), y_b.reshape(b * t, BRANCH_WIDTH), y_c, g_c, y_d,
                    gates, w_branch[i].astype(BF16), w_out[i].astype(BF16))
        x2 = _ffn(x2, norm_ffn_g[i], ffn_w1[i].astype(BF16), ffn_w3[i].astype(BF16), ffn_w2[i].astype(BF16),
                  norm_final_g, final_norm=(i == DEPTH - 1))
    return x2.reshape(b, t, D_MODEL)
```

```python
import functools
import math

import jax
import jax.numpy as jnp
import numpy as np
from jax import lax
from jax.experimental import pallas as pl
from jax.experimental.pallas import tpu as pltpu

D_MODEL = 1024
DEPTH = 4
CHUNK = 64
N_BRANCH = 4
BRANCH_WIDTH = 512
HEAD_DIM = 64
N_HEADS = 8
RMS_EPS = 1e-6

SSD_GROUPS = 2
SSD_STATE = 64
SSD_CONV = 4
SSD_XBC = 768

IDX_HEADS = 4
IDX_DIM = 32
DSA_MAX_TOPK = 256
REL_BUCKETS = 32
REL_MAX_DIST = 256

RWKV_COLS = 1792
RWKV_GN_EPS = 64e-5

S5_GROUP = 16
S5_GROUPS = 32
S5_STATE = 64

FFN_HIDDEN = 2816

LANES = 128
SUBLANES = 8
VMEM_LIMIT = 56 * 1024 * 1024

F32 = jnp.float32
BF16 = jnp.bfloat16
HIGHEST = lax.Precision.HIGHEST


def _params(*sem):
    return pltpu.CompilerParams(dimension_semantics=sem, vmem_limit_bytes=VMEM_LIMIT)


def _resident(shape):
    nd = len(shape)
    return pl.BlockSpec(shape, lambda *_: (0,) * nd, pipeline_mode=pl.Buffered(1))


def _dot(a, b):
    return jnp.dot(a, b, preferred_element_type=F32)


def _dot_exact(a, b):
    return jnp.dot(a, b, preferred_element_type=F32, precision=HIGHEST)


def _sigmoid(x):
    return 1.0 / (1.0 + jnp.exp(-x))


def _silu(x):
    return x * _sigmoid(x)


def _softplus(x):
    return jnp.maximum(x, 0.0) + jnp.log(1.0 + jnp.exp(-jnp.abs(x)))


PROJ_SEGS = (
    ("z", 512, F32), ("xbc", 768, F32), ("q", 512, BF16), ("k", 512, BF16), ("v", 512, BF16),
    ("qi", 128, F32), ("small", 128, F32), ("kidx", 128, BF16), ("rwkv", RWKV_COLS, F32), ("s5", 512, F32),
    ("gates", 4096, F32),
)
PROJ_COLS = sum(s[1] for s in PROJ_SEGS)


def _pack_w_in(w):
    o = np.cumsum((0, 512, 768, 8, 512, 512, 512, 128, 32, 4, RWKV_COLS, 512, 4096))
    seg = lambda i: w[:, o[i]:o[i + 1]]
    small = jnp.concatenate([seg(2), seg(7), seg(8), jnp.zeros((D_MODEL, LANES - 44), w.dtype)], axis=1)
    kidx = jnp.tile(seg(7), (1, IDX_HEADS))
    parts = [seg(0), seg(1), seg(3), seg(4), seg(5), seg(6), small, kidx, seg(9), seg(10), seg(11)]
    return jnp.concatenate(parts, axis=1).astype(BF16)


def _proj_kernel(x_ref, g_ref, w_ref, *out_refs):
    x = x_ref[...]
    ms = jnp.mean(x * x, axis=-1, keepdims=True)
    h = (x * lax.rsqrt(ms + RMS_EPS) * g_ref[...]).astype(BF16)
    off = 0
    for o_ref, (_, n, _) in zip(out_refs, PROJ_SEGS):
        o_ref[...] = _dot(h, w_ref[:, off:off + n]).astype(o_ref.dtype)
        off += n


def _proj(x2, g, w_packed, tm=256):
    m = x2.shape[0]
    tm = min(tm, m)
    return pl.pallas_call(
        _proj_kernel,
        grid=(m // tm,),
        in_specs=[pl.BlockSpec((tm, D_MODEL), lambda i: (i, 0)),
                  _resident((1, D_MODEL)), _resident((D_MODEL, PROJ_COLS))],
        out_specs=[pl.BlockSpec((tm, n), lambda i: (i, 0)) for _, n, _ in PROJ_SEGS],
        out_shape=[jax.ShapeDtypeStruct((m, n), dt) for _, n, dt in PROJ_SEGS],
        compiler_params=_params("parallel"),
        name="proj",
    )(x2, g.reshape(1, D_MODEL), w_packed)


def _merge_kernel(x_ref, ya_ref, yb_ref, yc_ref, gc_ref, yd_ref, gate_ref, wb_ref, wo_ref, o_ref):
    ys = (ya_ref[...], yb_ref[...], (yc_ref[...] * gc_ref[...]).astype(BF16), yd_ref[...])
    merged = None
    for n, y in enumerate(ys):
        gate = _sigmoid(gate_ref[:, n * D_MODEL:(n + 1) * D_MODEL])
        term = gate * _dot(y, wb_ref[n])
        merged = term if merged is None else merged + term
    o_ref[...] = x_ref[...] + _dot(merged.astype(BF16), wo_ref[...])


def _merge(x2, y_a, y_b, y_c, g_c, y_d, gates, wb, wo, tm=256):
    m = x2.shape[0]
    tm = min(tm, m)
    row = lambda n: pl.BlockSpec((tm, n), lambda i: (i, 0))
    return pl.pallas_call(
        _merge_kernel,
        grid=(m // tm,),
        in_specs=[row(D_MODEL)] + [row(BRANCH_WIDTH)] * 5 + [row(N_BRANCH * D_MODEL),
                  _resident((N_BRANCH, BRANCH_WIDTH, D_MODEL)), _resident((D_MODEL, D_MODEL))],
        out_specs=row(D_MODEL),
        out_shape=jax.ShapeDtypeStruct((m, D_MODEL), F32),
        compiler_params=_params("parallel"),
        name="merge",
    )(x2, y_a, y_b, y_c, g_c, y_d, gates, wb, wo)


def _ffn_kernel(x_ref, g_ref, w1_ref, w3_ref, w2_ref, gf_ref, o_ref, *, final_norm):
    x = x_ref[...]
    ms = jnp.mean(x * x, axis=-1, keepdims=True)
    h = (x * lax.rsqrt(ms + RMS_EPS) * g_ref[...]).astype(BF16)
    hid = _silu(_dot(h, w1_ref[...])) * _dot(h, w3_ref[...])
    x = x + _dot(hid.astype(BF16), w2_ref[...])
    if final_norm:
        ms = jnp.mean(x * x, axis=-1, keepdims=True)
        x = x * lax.rsqrt(ms + RMS_EPS) * gf_ref[...]
    o_ref[...] = x


def _ffn(x2, g, w1, w3, w2, g_final, final_norm, tm=256):
    m = x2.shape[0]
    tm = min(tm, m)
    row = pl.BlockSpec((tm, D_MODEL), lambda i: (i, 0))
    return pl.pallas_call(
        functools.partial(_ffn_kernel, final_norm=final_norm),
        grid=(m // tm,),
        in_specs=[row, _resident((1, D_MODEL)), _resident((D_MODEL, FFN_HIDDEN)),
                  _resident((D_MODEL, FFN_HIDDEN)), _resident((FFN_HIDDEN, D_MODEL)),
                  _resident((1, D_MODEL))],
        out_specs=row,
        out_shape=jax.ShapeDtypeStruct((m, D_MODEL), F32),
        compiler_params=_params("parallel"),
        name="ffn",
    )(x2, g.reshape(1, D_MODEL), w1, w3, w2, g_final.reshape(1, D_MODEL))


def _head_expand():
    e = np.zeros((LANES, BRANCH_WIDTH), np.float32)
    for h in range(N_HEADS):
        e[h, h * HEAD_DIM:(h + 1) * HEAD_DIM] = 1.0
    return jnp.asarray(e)


def _head_select():
    s = np.zeros((N_HEADS, BRANCH_WIDTH), np.float32)
    for h in range(N_HEADS):
        s[h, h * HEAD_DIM] = 1.0
    return jnp.asarray(s)


def _ssd_kernel(z_ref, xbc_ref, xprev_ref, small_ref, cw_ref, cb_ref, dtb_ref, alog_ref, dsk_ref,
                ng_ref, exp_ref, sel_ref, o_ref, xpad_scr, state_scr, y_scr, *, L):
    c = pl.program_id(1)

    @pl.when(c == 0)
    def _():
        state_scr[...] = jnp.zeros_like(state_scr)

    xpad_scr[0:SUBLANES, :] = jnp.where(c == 0, 0.0, xprev_ref[0])
    xpad_scr[SUBLANES:SUBLANES + L, :] = xbc_ref[0]
    acc = cb_ref[...]
    for j in range(SSD_CONV):
        lo = SUBLANES - (SSD_CONV - 1) + j
        acc = acc + cw_ref[j:j + 1, :] * xpad_scr[lo:lo + L, :]
    xc = _silu(acc)
    xs = xc[:, :BRANCH_WIDTH]

    dt = _softplus(_dot_exact(small_ref[0], exp_ref[...]) + dtb_ref[...])
    a = dt * (-jnp.exp(alog_ref[...]))
    rows = lax.broadcasted_iota(jnp.int32, (L, L), 0)
    cols = lax.broadcasted_iota(jnp.int32, (L, L), 1)
    causal = rows >= cols
    acs = _dot_exact(causal.astype(F32), a)
    acs_t = lax.dot_general(sel_ref[...], acs, (((1,), (1,)), ((), ())),
                            precision=HIGHEST, preferred_element_type=F32)
    xdt = xs * dt

    for g in range(SSD_GROUPS):
        b_g = xc[:, BRANCH_WIDTH + g * SSD_STATE:BRANCH_WIDTH + (g + 1) * SSD_STATE].astype(BF16)
        c_g = xc[:, BRANCH_WIDTH + (SSD_GROUPS + g) * SSD_STATE:
                 BRANCH_WIDTH + (SSD_GROUPS + g + 1) * SSD_STATE].astype(BF16)
        cb = lax.dot_general(c_g, b_g, (((1,), (1,)), ((), ())), preferred_element_type=F32)
        for hh in range(N_HEADS // SSD_GROUPS):
            h = g * (N_HEADS // SSD_GROUPS) + hh
            sl = slice(h * HEAD_DIM, (h + 1) * HEAD_DIM)
            col = acs[:, h * HEAD_DIM:h * HEAD_DIM + 1]
            row = acs_t[h:h + 1, :]
            dec = jnp.where(causal, jnp.exp(jnp.minimum(col - row, 0.0)), 0.0)
            x_h = xdt[:, sl]
            st = state_scr[h]
            y_h = _dot((cb * dec).astype(BF16), x_h.astype(BF16))
            y_h = y_h + _dot(c_g, st.astype(BF16)) * jnp.exp(col)
            y_scr[:, sl] = y_h
            last = acs[L - 1:L, h * HEAD_DIM:h * HEAD_DIM + 1]
            x_w = (x_h * jnp.exp(last - col)).astype(BF16)
            state_scr[h] = st * jnp.exp(last) + lax.dot_general(
                b_g, x_w, (((0,), (0,)), ((), ())), preferred_element_type=F32)

    y = y_scr[...] + dsk_ref[...] * xs
    y = y * _silu(z_ref[0])
    half = BRANCH_WIDTH // SSD_GROUPS
    for g in range(SSD_GROUPS):
        yg = y[:, g * half:(g + 1) * half]
        yg = yg * lax.rsqrt(jnp.mean(yg * yg, axis=-1, keepdims=True) + RMS_EPS)
        o_ref[0, :, g * half:(g + 1) * half] = (yg * ng_ref[:, g * half:(g + 1) * half]).astype(o_ref.dtype)


def _ssd(z, xbc, small, conv_w, conv_b, dt_bias, a_log, d_skip, norm_g, L=256):
    b, t, _ = z.shape
    rep = lambda p: jnp.repeat(p.astype(F32), HEAD_DIM).reshape(1, BRANCH_WIDTH)
    blk = lambda n: pl.BlockSpec((1, L, n), lambda i, j: (i, j, 0))
    prev = pl.BlockSpec((1, SUBLANES, SSD_XBC),
                        lambda i, j: (i, jnp.maximum(j * (L // SUBLANES) - 1, 0), 0))
    return pl.pallas_call(
        functools.partial(_ssd_kernel, L=L),
        grid=(b, t // L),
        in_specs=[blk(BRANCH_WIDTH), blk(SSD_XBC), prev, blk(LANES),
                  _resident((SSD_CONV, SSD_XBC)), _resident((1, SSD_XBC)),
                  _resident((1, BRANCH_WIDTH)), _resident((1, BRANCH_WIDTH)), _resident((1, BRANCH_WIDTH)),
                  _resident((1, BRANCH_WIDTH)), _resident((LANES, BRANCH_WIDTH)),
                  _resident((N_HEADS, BRANCH_WIDTH))],
        out_specs=blk(BRANCH_WIDTH),
        out_shape=jax.ShapeDtypeStruct((b, t, BRANCH_WIDTH), BF16),
        scratch_shapes=[pltpu.VMEM((SUBLANES + L, SSD_XBC), F32),
                        pltpu.VMEM((N_HEADS, SSD_STATE, HEAD_DIM), F32),
                        pltpu.VMEM((L, BRANCH_WIDTH), F32)],
        compiler_params=_params("parallel", "arbitrary"),
        name="ssd",
    )(z, xbc, xbc, small, conv_w, conv_b.reshape(1, SSD_XBC), rep(dt_bias), rep(a_log), rep(d_skip),
      norm_g.reshape(1, BRANCH_WIDTH), _head_expand(), _head_select())


S5_CHUNK = 64
S5_ROW = S5_CHUNK * S5_GROUP


def _s5_tables(a_re, a_im, b_re, b_im, c_re, c_im, log_dt):
    L = S5_CHUNK
    f = lambda v: v.astype(F32)
    a_re, a_im, b_re, b_im, c_re, c_im = map(f, (a_re, a_im, b_re, b_im, c_re, c_im))
    dt = jnp.exp(f(log_dt))[:, None]
    lr, li = a_re * dt, a_im * dt
    tau = jnp.arange(L + 1, dtype=F32)[None, :, None]
    mag = jnp.exp(lr[:, None, :] * tau)
    pr, pi = mag * jnp.cos(li[:, None, :] * tau), mag * jnp.sin(li[:, None, :] * tau)
    abr, abi = pr[:, 1], pi[:, 1]
    den = a_re * a_re + a_im * a_im
    qr = ((abr - 1.0) * a_re + abi * a_im) / den
    qi = (abi * a_re - (abr - 1.0) * a_im) / den
    bbr = qr[..., None] * b_re - qi[..., None] * b_im
    bbi = qr[..., None] * b_im + qi[..., None] * b_re
    pbr = pr[..., None] * bbr[:, None] - pi[..., None] * bbi[:, None]
    pbi = pr[..., None] * bbi[:, None] + pi[..., None] * bbr[:, None]
    kern = (jnp.einsum("gop,gtpi->gtoi", c_re, pbr[:, :L], precision=HIGHEST)
            - jnp.einsum("gop,gtpi->gtoi", c_im, pbi[:, :L], precision=HIGHEST))
    tz = _s5_toeplitz(kern.transpose(0, 3, 1, 2).reshape(S5_GROUPS, S5_GROUP, S5_ROW))
    rev = L - 1 - jnp.arange(L)
    ws = jnp.concatenate([pbr[:, rev], pbi[:, rev]], axis=2)
    ws = ws.transpose(0, 1, 3, 2).reshape(S5_GROUPS, S5_ROW, 2 * S5_STATE)
    cpr = c_re[:, None] * pr[:, 1:, None, :] - c_im[:, None] * pi[:, 1:, None, :]
    cpi = c_re[:, None] * pi[:, 1:, None, :] + c_im[:, None] * pr[:, 1:, None, :]
    wy = jnp.concatenate([cpr, -cpi], axis=3).transpose(0, 3, 1, 2).reshape(S5_GROUPS, 2 * S5_STATE, S5_ROW)
    a1 = jnp.concatenate([pr[:, L], pr[:, L]], axis=1)[:, None, :]
    a2 = jnp.concatenate([-pi[:, L], pi[:, L]], axis=1)[:, None, :]
    return tz, ws.astype(BF16), wy.astype(BF16), a1, a2


def _s5_toeplitz_kernel(k_ref, o_ref):
    kf = k_ref[0]
    col = lax.broadcasted_iota(jnp.int32, kf.shape, 1)
    for s in range(S5_CHUNK):
        shifted = pltpu.roll(kf, s * S5_GROUP, axis=1) if s else kf
        o_ref[0, s * S5_GROUP:(s + 1) * S5_GROUP, :] = jnp.where(col >= s * S5_GROUP, shifted, 0.0).astype(BF16)


def _s5_toeplitz(kflat):
    return pl.pallas_call(
        _s5_toeplitz_kernel,
        grid=(S5_GROUPS,),
        in_specs=[pl.BlockSpec((1, S5_GROUP, S5_ROW), lambda g: (g, 0, 0))],
        out_specs=pl.BlockSpec((1, S5_ROW, S5_ROW), lambda g: (g, 0, 0)),
        out_shape=jax.ShapeDtypeStruct((S5_GROUPS, S5_ROW, S5_ROW), BF16),
        compiler_params=_params("parallel"),
        name="s5_toeplitz",
    )(kflat)


def _s5_kernel(u_ref, tz_ref, ws_ref, wy_ref, a1_ref, a2_ref, y_ref, sl_scr, x_scr, *, nb, nc):
    u = u_ref[0]
    y_local = _dot(u, tz_ref[0])
    sl_scr[...] = _dot(u, ws_ref[0])
    a1, a2 = a1_ref[0], a2_ref[0]

    def step(c, x):
        r = pl.ds(pl.multiple_of(c * nb, SUBLANES), nb)
        x_scr[r, :] = x
        return a1 * x + a2 * pltpu.roll(x, S5_STATE, axis=1) + sl_scr[r, :]

    lax.fori_loop(0, nc, step, jnp.zeros((nb, 2 * S5_STATE), F32))
    y_ref[0] = y_local + _dot(x_scr[...].astype(BF16), wy_ref[0])


def _s5_out_kernel(y_ref, u_ref, d_ref, w_ref, b_ref, o_ref):
    y = y_ref[...] + d_ref[...] * u_ref[...]
    y = 0.5 * y * (1.0 + jnp.tanh(math.sqrt(2.0 / math.pi) * (y + 0.044715 * (y * y * y))))
    o_ref[...] = (y * _sigmoid(_dot(y.astype(BF16), w_ref[...]) + b_ref[...])).astype(o_ref.dtype)


def _s5(u, tables, d_skip, glu_w, glu_b, tm=512):
    b, t, _ = u.shape
    nc = t // S5_CHUNK
    m = nc * b
    tz, ws, wy, a1, a2 = tables
    ug = u.reshape(b, nc, S5_CHUNK, S5_GROUPS, S5_GROUP).transpose(3, 1, 0, 2, 4)
    ug = ug.reshape(S5_GROUPS, m, S5_ROW).astype(BF16)
    grp = lambda r, c: pl.BlockSpec((1, r, c), lambda g: (g, 0, 0))
    y = pl.pallas_call(
        functools.partial(_s5_kernel, nb=b, nc=nc),
        grid=(S5_GROUPS,),
        in_specs=[grp(m, S5_ROW), grp(S5_ROW, S5_ROW), grp(S5_ROW, 2 * S5_STATE),
                  grp(2 * S5_STATE, S5_ROW), grp(1, 2 * S5_STATE), grp(1, 2 * S5_STATE)],
        out_specs=grp(m, S5_ROW),
        out_shape=jax.ShapeDtypeStruct((S5_GROUPS, m, S5_ROW), F32),
        scratch_shapes=[pltpu.VMEM((m, 2 * S5_STATE), F32), pltpu.VMEM((m, 2 * S5_STATE), F32)],
        compiler_params=_params("parallel"),
        name="s5_scan",
    )(ug, tz, ws, wy, a1, a2)
    y = y.reshape(S5_GROUPS, nc, b, S5_CHUNK, S5_GROUP).transpose(2, 1, 3, 0, 4).reshape(b * t, BRANCH_WIDTH)
    tm = min(tm, b * t)
    row = pl.BlockSpec((tm, BRANCH_WIDTH), lambda i: (i, 0))
    return pl.pallas_call(
        _s5_out_kernel,
        grid=(b * t // tm,),
        in_specs=[row, row, _resident((1, BRANCH_WIDTH)), _resident((BRANCH_WIDTH, BRANCH_WIDTH)),
                  _resident((1, BRANCH_WIDTH))],
        out_specs=row,
        out_shape=jax.ShapeDtypeStruct((b * t, BRANCH_WIDTH), BF16),
        compiler_params=_params("parallel"),
        name="s5_out",
    )(y, u.reshape(b * t, BRANCH_WIDTH), d_skip.reshape(1, BRANCH_WIDTH).astype(F32), glu_w.astype(BF16),
      glu_b.reshape(1, BRANCH_WIDTH).astype(F32))


RWKV_LORA_OFF = 3 * BRANCH_WIDTH
RWKV_GATE_OFF = RWKV_LORA_OFF + LANES


def _rwkv_pre_kernel(p_ref, prev_ref, mu_ref, w0_ref, w2_ref, a0_ref, a2_ref, g2_ref, kk_ref, ka_ref,
                     r_o, w_o, k_o, v_o, kk_o, a_o, g_o, pad_scr, *, tm):
    j = pl.program_id(1)
    pad_scr[0:SUBLANES, :] = jnp.where(j == 0, 0.0, prev_ref[0])
    pad_scr[SUBLANES:SUBLANES + tm, :] = p_ref[0]
    p = p_ref[0]
    p = p + (pad_scr[SUBLANES - 1:SUBLANES - 1 + tm, :] - p) * mu_ref[...]
    r = p[:, 0:BRANCH_WIDTH]
    k = p[:, BRANCH_WIDTH:2 * BRANCH_WIDTH]
    v = p[:, 2 * BRANCH_WIDTH:3 * BRANCH_WIDTH]
    lora = p[:, RWKV_LORA_OFF:RWKV_GATE_OFF]
    dg = p[:, RWKV_GATE_OFF:RWKV_COLS]
    w = -_softplus(-(w0_ref[...] + _dot(jnp.tanh(lora).astype(BF16), w2_ref[...]))) - 0.5
    a = _sigmoid(a0_ref[...] + _dot(lora.astype(BF16), a2_ref[...]))
    r_o[0] = r
    w_o[0] = jnp.exp(-jnp.exp(w))
    k_o[0] = k * (1.0 + (a - 1.0) * ka_ref[...])
    v_o[0] = v
    kk_o[0] = k * kk_ref[...]
    a_o[0] = a
    g_o[0] = _dot(_sigmoid(dg).astype(BF16), g2_ref[...])


def _rwkv_scan_kernel(r_ref, w_ref, k_ref, v_ref, kk_ref, a_ref, rk_ref, lng_ref, lnb_ref, y_ref,
                      s_scr, even_scr, odd_scr, *, tb):
    nl = s_scr.shape[2]

    @pl.when(pl.program_id(0) == 0)
    def _():
        s_scr[...] = jnp.zeros_like(s_scr)

    def stage(t, buf):
        def tile(ref):
            x = ref[:, t]
            rows = jnp.concatenate([x[:, h * HEAD_DIM:(h + 1) * HEAD_DIM] for h in range(N_HEADS)], axis=0)
            return rows.T
        kk = tile(kk_ref)
        kk = kk / jnp.maximum(jnp.sqrt(jnp.sum(kk * kk, axis=0, keepdims=True)), 1e-12)
        for i, x in enumerate((tile(r_ref), tile(w_ref), tile(k_ref), tile(v_ref), -kk, kk * tile(a_ref))):
            buf[i] = x

    def step(t, buf):
        row = lambda i, kx: buf[i, pl.ds(kx, 1), :]

        def s_times_a(kb, sa):
            for j in range(SUBLANES):
                kx = kb * SUBLANES + j
                sa = sa + s_scr[kx] * row(4, kx)
            return sa

        sa = jnp.zeros(s_scr.shape[1:], F32)
        for kb in range(HEAD_DIM // SUBLANES):
            sa = s_times_a(kb, sa)
        v_t = buf[3]

        def update(kb, y):
            for j in range(SUBLANES):
                kx = kb * SUBLANES + j
                s_new = s_scr[kx] * row(1, kx) + sa * row(5, kx) + v_t * row(2, kx)
                s_scr[kx] = s_new
                y = y + s_new * row(0, kx)
            return y

        y = lax.fori_loop(0, HEAD_DIM // SUBLANES, update, jnp.zeros(s_scr.shape[1:], F32))
        mean = jnp.mean(y, axis=0, keepdims=True)
        var = jnp.mean(jnp.square(y - mean), axis=0, keepdims=True)
        y = (y - mean) * lax.rsqrt(var + RWKV_GN_EPS) * lng_ref[0] + lnb_ref[0]
        y = y + jnp.sum(buf[0] * buf[2] * rk_ref[0], axis=0, keepdims=True) * v_t
        y_t = y.T
        nb = nl // N_HEADS
        y_ref[:, t] = jnp.concatenate([y_t[h * nb:(h + 1) * nb] for h in range(N_HEADS)], axis=1)

    def two_steps(i, carry):
        t = 2 * i
        stage(t + 1, odd_scr)
        step(t, even_scr)
        stage(jnp.minimum(t + 2, tb - 1), even_scr)
        step(t + 1, odd_scr)
        return carry

    stage(0, even_scr)
    lax.fori_loop(0, tb // 2, two_steps, 0)


def _rwkv(cols, mu, w0, w2, a0, a2, g2, k_k, k_a, r_k, ln_g, ln_b, tm=256, tb=32):
    b, t, _ = cols.shape
    tm, tb = min(tm, t), min(tb, t)
    vec = lambda p: p.reshape(1, -1).astype(F32)
    zpad = jnp.zeros((LANES // 2, BRANCH_WIDTH), F32)
    blk = lambda n: pl.BlockSpec((1, tm, n), lambda i, j: (i, j, 0))
    prev = pl.BlockSpec((1, SUBLANES, RWKV_COLS),
                        lambda i, j: (i, jnp.maximum(j * (tm // SUBLANES) - 1, 0), 0))
    outs = pl.pallas_call(
        functools.partial(_rwkv_pre_kernel, tm=tm),
        grid=(b, t // tm),
        in_specs=[blk(RWKV_COLS), prev, _resident((1, RWKV_COLS)), _resident((1, BRANCH_WIDTH)),
                  _resident((LANES, BRANCH_WIDTH)), _resident((1, BRANCH_WIDTH)),
                  _resident((LANES, BRANCH_WIDTH)), _resident((LANES, BRANCH_WIDTH)),
                  _resident((1, BRANCH_WIDTH)), _resident((1, BRANCH_WIDTH))],
        out_specs=[blk(BRANCH_WIDTH)] * 7,
        out_shape=[jax.ShapeDtypeStruct((b, t, BRANCH_WIDTH), F32)] * 7,
        scratch_shapes=[pltpu.VMEM((SUBLANES + tm, RWKV_COLS), F32)],
        compiler_params=_params("parallel", "arbitrary"),
        name="rwkv_pre",
    )(cols, cols, vec(mu), vec(w0), jnp.concatenate([w2.astype(F32), zpad]).astype(BF16), vec(a0),
      jnp.concatenate([zpad, a2.astype(F32)]).astype(BF16), g2.astype(BF16), vec(k_k), vec(k_a))
    *seqs, gate = outs
    per_head = lambda p: jnp.repeat(p.astype(F32).reshape(N_HEADS, HEAD_DIM).T, b, axis=1)[None]
    nl = b * N_HEADS
    heads = lambda s: s
    tblk = pl.BlockSpec((b, tb, BRANCH_WIDTH), lambda i: (0, i, 0))
    tiles = pltpu.VMEM((6, HEAD_DIM, nl), F32)
    y = pl.pallas_call(
        functools.partial(_rwkv_scan_kernel, tb=tb),
        grid=(t // tb,),
        in_specs=[tblk] * 6 + [_resident((1, HEAD_DIM, nl))] * 3,
        out_specs=tblk,
        out_shape=jax.ShapeDtypeStruct((b, t, BRANCH_WIDTH), F32),
        scratch_shapes=[pltpu.VMEM((HEAD_DIM, HEAD_DIM, nl), F32), tiles, tiles],
        compiler_params=_params("arbitrary"),
        name="rwkv_scan",
    )(*[heads(s) for s in seqs], per_head(r_k), per_head(ln_g), per_head(ln_b))
    return y.reshape(b * t, BRANCH_WIDTH), gate.reshape(b * t, BRANCH_WIDTH)


DSA_TQ = 128
DSA_TK = 512
INT_MIN = -2 ** 31
MASKED = -1e30


def _bias_tiles():
    n = np.arange(1, 8192, dtype=np.float32)
    large = 8 + (np.log(n / np.float32(8)) / np.float32(math.log(REL_MAX_DIST / 8)) * np.float32(8)).astype(np.int32)
    n_sat = int(n[np.argmax(large >= 15)])
    return -(-(n_sat + DSA_TQ - 1) // DSA_TQ) + 1


def _bias_kernel(rb_ref, o_ref):
    e = pl.program_id(0)
    t = DSA_TQ
    rel = (lax.broadcasted_iota(jnp.int32, (t, t), 0) - lax.broadcasted_iota(jnp.int32, (t, t), 1)) - e * t
    half, exact = REL_BUCKETS // 2, REL_BUCKETS // 4
    n = jnp.abs(rel)
    n_f = jnp.maximum(n, 1).astype(F32)
    large = exact + (jnp.log(n_f / exact) / math.log(REL_MAX_DIST / exact) * (half - exact)).astype(jnp.int32)
    large = jnp.minimum(large, half - 1)
    bucket = jnp.where(rel > 0, half, 0) + jnp.where(n < exact, n, large)
    for h in range(N_HEADS):
        acc = jnp.zeros((t, t), F32)
        for bkt in range(REL_BUCKETS):
            acc = jnp.where(bucket == bkt, rb_ref[bkt, h], acc)
        o_ref[h // 2, 0, :, (h % 2) * t:(h % 2 + 1) * t] = acc


def _rel_bias_table(rel_bias):
    nd = _bias_tiles()
    t = DSA_TQ
    return pl.pallas_call(
        _bias_kernel,
        grid=(nd,),
        in_specs=[pl.BlockSpec(memory_space=pltpu.SMEM)],
        out_specs=pl.BlockSpec((N_HEADS // 2, 1, t, 2 * t), lambda e: (0, e, 0, 0)),
        out_shape=jax.ShapeDtypeStruct((N_HEADS // 2, nd, t, 2 * t), F32),
        compiler_params=_params("parallel"),
        name="rel_bias",
    )(rel_bias.astype(F32))


def _nt_dot(a, b, **kw):
    return lax.dot_general(a, b, (((1,), (1,)), ((), ())), preferred_element_type=F32, **kw)


def _dsa_kernel(q_ref, k_ref, vt_ref, qi_ref, kidx_ref, small_ref, bias_ref, tri_ref, pick_ref, o_ref,
                key_scr, msk_scr, lga_scr, lgb_scr, *, topk, nd):
    tq, tk = DSA_TQ, DSA_TK
    th = tk // 2
    q0 = pl.program_id(1) * tq
    nkb = (q0 + tq + tk - 1) // tk
    int_min = jnp.int32(INT_MIN)
    q_pos = q0 + lax.broadcasted_iota(jnp.int32, (1, tq), 1)
    lim = (q_pos // CHUNK + 1) * CHUNK
    lane = lax.broadcasted_iota(jnp.int32, (tq, LANES), 1)

    qi = qi_ref[0]
    qi_all = jnp.concatenate(
        [jnp.where((lane >= h * IDX_DIM) & (lane < (h + 1) * IDX_DIM), qi, 0.0).astype(BF16)
         for h in range(IDX_HEADS)], axis=0)
    w_t = _nt_dot(pick_ref[...], small_ref[0], precision=HIGHEST)
    w_h = [w_t[h:h + 1, :] for h in range(IDX_HEADS)]

    def score_block(j, carry):
        s0 = pl.multiple_of(j * tk, tk)
        kidx = kidx_ref[0, pl.ds(s0, tk), :]
        dots = _nt_dot(kidx, qi_all)
        s = None
        for h in range(IDX_HEADS):
            term = w_h[h] * jnp.maximum(dots[:, h * tq:(h + 1) * tq], 0.0)
            s = term if s is None else s + term
        s = s + 0.0
        bits = lax.bitcast_convert_type(s, jnp.int32)
        bits = jnp.where(bits < 0, bits ^ jnp.int32(0x7FFFFFFF), bits)
        pos = s0 + lax.broadcasted_iota(jnp.int32, (tk, tq), 0)
        key_scr[pl.ds(s0, tk), :] = jnp.where(pos < lim, bits, int_min)
        return carry

    lax.fori_loop(0, nkb, score_block, 0)

    acc_rows = 4 * SUBLANES

    def count(pred):
        def body(j, acc):
            hit = jnp.where(pred(key_scr[pl.ds(pl.multiple_of(j * tk, tk), tk), :]), 1.0, 0.0)
            return acc + jnp.sum(hit.reshape(tk // acc_rows, acc_rows, tq), axis=0)
        acc = lax.fori_loop(0, nkb, body, jnp.zeros((acc_rows, tq), F32))
        return jnp.sum(acc, axis=0, keepdims=True)

    kf = jnp.float32(topk)
    c0 = count(lambda kb: kb >= 0)
    thr = jnp.where(c0 >= kf, jnp.int32(0), int_min)
    n_ge = jnp.where(c0 >= kf, c0, 0.0)

    def search(it, carry):
        thr, n_ge = carry
        cand = thr + jnp.left_shift(jnp.int32(1), 30 - it)
        c = count(lambda kb: kb >= cand)
        return jnp.where(c >= kf, cand, thr), jnp.where(c >= kf, c, n_ge)

    thr, n_ge = lax.fori_loop(0, 31, search, (thr, n_ge))
    real = thr > int_min
    surplus = jnp.max(jnp.where(real, n_ge - kf, 0.0))

    def select_plain(j, carry):
        s0 = pl.multiple_of(j * tk, tk)
        kb = key_scr[pl.ds(s0, tk), :]
        msk_scr[pl.ds(s0, tk), :] = jnp.where((kb >= thr) & (kb > int_min), 0.0, MASKED)
        return carry

    @pl.when(surplus <= 0.0)
    def _():
        lax.fori_loop(0, nkb, select_plain, 0)

    @pl.when(surplus > 0.0)
    def _():
        quota = kf - count(lambda kb: kb > thr)

        def select_ranked(j, run):
            s0 = pl.multiple_of(j * tk, tk)
            kb = key_scr[pl.ds(s0, tk), :]
            eq = kb == thr
            rank = run + _dot(tri_ref[...], jnp.where(eq, 1.0, 0.0).astype(BF16))
            sel = (kb > thr) | (eq & real & (rank <= quota))
            msk_scr[pl.ds(s0, tk), :] = jnp.where(sel, 0.0, MASKED)
            return rank[tk - 1:tk, :]

        lax.fori_loop(0, nkb, select_ranked, jnp.zeros((1, tq), F32))

    q = q_ref[0]
    dim = lax.broadcasted_iota(jnp.int32, (LANES, tq), 0)
    for hp in range(N_HEADS // 2):
        q_pair = q[:, hp * LANES:(hp + 1) * LANES] * HEAD_DIM ** -0.5
        q2 = jnp.concatenate([jnp.where(lane < HEAD_DIM, q_pair, 0.0).astype(BF16),
                              jnp.where(lane >= HEAD_DIM, q_pair, 0.0).astype(BF16)], axis=0)

        def raw_logits(jh, hp=hp, q2=q2):
            s0 = pl.multiple_of(jh * th, th)
            return _nt_dot(k_ref[0, pl.ds(s0, th), hp * LANES:(hp + 1) * LANES], q2)

        def finish_logits(lg, jh, buf, hp=hp):
            s0 = pl.multiple_of(jh * th, th)
            e0 = (q0 - s0) // tq
            top = None
            for c in range(th // LANES):
                mask = msk_scr[pl.ds(s0 + c * LANES, LANES), :]
                lg_c = (lg[c * LANES:(c + 1) * LANES] + bias_ref[hp, jnp.clip(e0 - c, 0, nd - 1)]
                        + jnp.concatenate([mask, mask], axis=1))
                buf[c * LANES:(c + 1) * LANES, :] = lg_c
                top_c = jnp.max(lg_c, axis=0, keepdims=True)
                top = top_c if top is None else jnp.maximum(top, top_c)
            return top

        def softmax_pv(state, buf, top, j, half, hp=hp):
            m, l, acc = state
            m_new = jnp.maximum(m, top)
            alpha = jnp.exp(m - m_new)
            p = jnp.exp(buf[...] - m_new)
            l = alpha * l + jnp.sum(p, axis=0, keepdims=True)
            vt_blk = vt_ref[0, j, hp * LANES:(hp + 1) * LANES, half * th:(half + 1) * th]
            return m_new, l, alpha * acc + _dot(vt_blk, p.astype(BF16))

        def attend(j, carry):
            *state, top_a, top_b = carry
            ja = jnp.minimum(2 * j + 2, 2 * nkb - 2)
            lg_a, lg_b = raw_logits(ja), raw_logits(ja + 1)
            state = softmax_pv(state, lga_scr, top_a, j, 0)
            state = softmax_pv(state, lgb_scr, top_b, j, 1)
            return (*state, finish_logits(lg_a, ja, lga_scr), finish_logits(lg_b, ja + 1, lgb_scr))

        init = (jnp.full((1, 2 * tq), MASKED, F32), jnp.zeros((1, 2 * tq), F32), jnp.zeros((LANES, 2 * tq), F32),
                finish_logits(raw_logits(0), 0, lga_scr), finish_logits(raw_logits(1), 1, lgb_scr))
        _, l, acc, _, _ = lax.fori_loop(0, nkb, attend, init)
        o2 = acc / l
        out_t = jnp.where(dim < HEAD_DIM, o2[:, :tq], o2[:, tq:])
        o_ref[0, :, hp * LANES:(hp + 1) * LANES] = out_t.T.astype(o_ref.dtype)


def _dsa(q, k, v, qi, kidx, small, bias_tab):
    b, t, _ = q.shape
    topk = min(DSA_MAX_TOPK, t // 4)
    nd = bias_tab.shape[1]
    nblk = t // DSA_TK
    vt = v.reshape(b, nblk, DSA_TK, BRANCH_WIDTH).transpose(0, 1, 3, 2)
    tri = jnp.asarray(np.tril(np.ones((DSA_TK, DSA_TK), np.float32))).astype(BF16)
    pick = np.zeros((SUBLANES, LANES), np.float32)
    pick[np.arange(IDX_HEADS), 40 + np.arange(IDX_HEADS)] = 1.0
    qblk = lambda n: pl.BlockSpec((1, DSA_TQ, n), lambda i, j: (i, j, 0))
    full = lambda *s: pl.BlockSpec((1,) + s, lambda i, j: (i,) + (0,) * len(s))
    return pl.pallas_call(
        functools.partial(_dsa_kernel, topk=topk, nd=nd),
        grid=(b, t // DSA_TQ),
        in_specs=[qblk(BRANCH_WIDTH), full(t, BRANCH_WIDTH), full(nblk, BRANCH_WIDTH, DSA_TK), qblk(LANES),
                  full(t, LANES), qblk(LANES), _resident(bias_tab.shape), _resident((DSA_TK, DSA_TK)),
                  _resident((SUBLANES, LANES))],
        out_specs=qblk(BRANCH_WIDTH),
        out_shape=jax.ShapeDtypeStruct((b, t, BRANCH_WIDTH), BF16),
        scratch_shapes=[pltpu.VMEM((t, DSA_TQ), jnp.int32), pltpu.VMEM((t, DSA_TQ), F32),
                        pltpu.VMEM((DSA_TK // 2, 2 * DSA_TQ), F32), pltpu.VMEM((DSA_TK // 2, 2 * DSA_TQ), F32)],
        compiler_params=_params("parallel", "arbitrary"),
        name="dsa",
    )(q, k, vt, qi, kidx, small, bias_tab, tri, jnp.asarray(pick))


def kernel(x, rel_bias, norm_mix_g, w_in, ssd_conv_w, ssd_conv_b, ssd_dt_bias, ssd_a_log, ssd_d, ssd_norm_g,
           rwkv_mu, rwkv_w0, rwkv_w2, rwkv_a0, rwkv_a2, rwkv_g2, rwkv_k_k, rwkv_k_a, rwkv_r_k, rwkv_ln_g,
           rwkv_ln_b, s5_a_re, s5_a_im, s5_b_re, s5_b_im, s5_c_re, s5_c_im, s5_d, s5_log_dt, s5_glu_w,
           s5_glu_b, w_branch, w_out, norm_ffn_g, ffn_w1, ffn_w3, ffn_w2, norm_final_g):
    b, t, _ = x.shape
    x2 = x.reshape(b * t, D_MODEL).astype(F32)
    seq = lambda a: a.reshape(b, t, a.shape[-1])
    bias_tab = _rel_bias_table(rel_bias)
    for i in range(DEPTH):
        z, xbc, q, k, v, qi, small, kidx, rwkv_cols, s5_u, gates = _proj(x2, norm_mix_g[i], _pack_w_in(w_in[i]))
        small = seq(small)
        y_a = _ssd(seq(z), seq(xbc), small, ssd_conv_w[i], ssd_conv_b[i], ssd_dt_bias[i], ssd_a_log[i],
                   ssd_d[i], ssd_norm_g[i])
        y_b = _dsa(seq(q), seq(k), seq(v), seq(qi), seq(kidx), small, bias_tab)
        y_c, g_c = _rwkv(seq(rwkv_cols), rwkv_mu[i], rwkv_w0[i], rwkv_w2[i], rwkv_a0[i], rwkv_a2[i],
                         rwkv_g2[i], rwkv_k_k[i], rwkv_k_a[i], rwkv_r_k[i], rwkv_ln_g[i], rwkv_ln_b[i])
        tables = _s5_tables(s5_a_re[i], s5_a_im[i], s5_b_re[i], s5_b_im[i], s5_c_re[i], s5_c_im[i],
                            s5_log_dt[i])
        y_d = _s5(seq(s5_u), tables, s5_d[i], s5_glu_w[i], s5_glu_b[i])
        x2 = _merge(x2, y_a.reshape(b * t, BRANCH_WIDTH), y_b.reshape(b * t, BRANCH_WIDTH), y_c, g_c, y_d,
                    gates, w_branch[i].astype(BF16), w_out[i].astype(BF16))
        x2 = _ffn(x2, norm_ffn_g[i], ffn_w1[i].astype(BF16), ffn_w3[i].astype(BF16), ffn_w2[i].astype(BF16),
                  norm_final_g, final_norm=(i == DEPTH - 1))
    return x2.reshape(b, t, D_MODEL)
```
